```python
import math
import jax, jax.numpy as jnp
from jax import lax
import numpy as np

D_MODEL = 4096
BATCH = 4
SEQ = 2048
DEPTH = 2
DEC_BATCH = 128
DEC_SEQ = 1
PAST_LEN = 16384
PAGE_SIZE = 128

D_FF = 11008
POOL_WINDOWS = (2, 4, 8, 16)
POOL_WIDTH = D_MODEL // 4
POOL_GROUP = POOL_WIDTH // len(POOL_WINDOWS)
POOL_BUF = max(POOL_WINDOWS) - 1
MLA_HEADS = 24
Q_LORA = D_MODEL // 4
KV_LORA = 512
QK_NOPE = 128
QK_ROPE = 64
V_HEAD = 128
ROPE_THETA = 10000.0
Q_BLOCK = 128
SM_SCALE = (QK_NOPE + QK_ROPE) ** -0.5
NEG_INF = -1e30
SSM_WIDTH = D_MODEL
SSM_GROUP = 16
SSM_GROUPS = SSM_WIDTH // SSM_GROUP
SSM_STATE = 64
DT_MIN = 1e-3
DT_MAX = 1e-1
MIX_IN0 = POOL_WIDTH + Q_LORA + KV_LORA + QK_ROPE
MIX_OUT0 = POOL_WIDTH + MLA_HEADS * V_HEAD
N_NORMS = 6
EPS = 1e-6

kernel_name = "hybrid_pool_mla_s5_macaron_step"


def rms_norm(x, g):
    xf = x.astype(jnp.float32)
    y = xf * lax.rsqrt(jnp.mean(xf * xf, axis=-1, keepdims=True) + EPS)
    return (y * g.astype(jnp.float32)).astype(x.dtype)


def swiglu(x, w_gate, w_up, w_down):
    return (jax.nn.silu(x @ w_gate) * (x @ w_up)) @ w_down


def rope(x, pos):
    half = QK_ROPE // 2
    inv_freq = ROPE_THETA ** (-jnp.arange(half, dtype=jnp.float32) / half)
    ang = pos.astype(jnp.float32)[:, None] * inv_freq[None, :]
    shape = (1, x.shape[1]) + (1,) * (x.ndim - 3) + (half,)
    cos = jnp.cos(ang).reshape(shape)
    sin = jnp.sin(ang).reshape(shape)
    xf = x.astype(jnp.float32)
    x1, x2 = xf[..., :half], xf[..., half:]
    return jnp.concatenate([x1 * cos - x2 * sin, x2 * cos + x1 * sin], axis=-1).astype(x.dtype)


def pool_mix(u, buf, start, w_pool, pool_scale):
    b, t, _ = u.shape
    ext = jnp.concatenate([buf.astype(u.dtype), u], axis=1).astype(jnp.float32)
    cs = jnp.concatenate([jnp.zeros((b, 1, POOL_WIDTH), jnp.float32), jnp.cumsum(ext, axis=1)], axis=1)
    pos = start + jnp.arange(t)
    incl = cs[:, POOL_BUF + 1:]
    means = []
    for gi, w in enumerate(POOL_WINDOWS):
        lo, hi = gi * POOL_GROUP, (gi + 1) * POOL_GROUP
        s = incl[..., lo:hi] - cs[:, POOL_BUF + 1 - w: POOL_BUF + 1 - w + t, lo:hi]
        cnt = jnp.minimum(pos + 1, w).astype(jnp.float32)
        means.append(s / cnt[None, :, None])
    d = jnp.concatenate(means, axis=-1) - ext[:, POOL_BUF:]
    d = d.astype(u.dtype).reshape(b, t, len(POOL_WINDOWS), POOL_GROUP)
    y = jnp.einsum('btgc,gcd->btgd', d, w_pool).reshape(b, t, POOL_WIDTH)
    return y * pool_scale, ext[:, -POOL_BUF:].astype(u.dtype)


def mla_prompt_attn(q_lat, q_pe, ckv, kpe):
    b, s = q_lat.shape[:2]
    nb = s // Q_BLOCK
    ql = q_lat.reshape(b, nb, Q_BLOCK, MLA_HEADS, KV_LORA).swapaxes(0, 1)
    qp = q_pe.reshape(b, nb, Q_BLOCK, MLA_HEADS, QK_ROPE).swapaxes(0, 1)
    kpos = jnp.arange(s)

    def block(args):
        ql_i, qp_i, i = args
        qpos = i * Q_BLOCK + jnp.arange(Q_BLOCK)
        sc = (jnp.einsum('bqhc,bkc->bhqk', ql_i, ckv)
              + jnp.einsum('bqhr,bkr->bhqk', qp_i, kpe)).astype(jnp.float32) * SM_SCALE
        sc = jnp.where(kpos[None, :] <= qpos[:, None], sc, NEG_INF)
        p = jax.nn.softmax(sc, axis=-1).astype(ckv.dtype)
        return jnp.einsum('bhqk,bkc->bqhc', p, ckv)

    o = lax.map(block, (ql, qp, jnp.arange(nb)))
    return o.swapaxes(0, 1).reshape(b, s, MLA_HEADS, KV_LORA)


def mla_sample_attn(q_lat, q_pe, ckv_new, kpe_new, cache_ckv, cache_kpe, page_table):
    t = q_lat.shape[1]
    causal = jnp.arange(t)[None, :] <= jnp.arange(t)[:, None]

    def one(args):
        ql, qp, cn, pn, pages = args
        past_c = cache_ckv[pages].reshape(-1, KV_LORA)
        past_p = cache_kpe[pages].reshape(-1, QK_ROPE)
        s_past = (jnp.einsum('qhc,kc->hqk', ql, past_c)
                  + jnp.einsum('qhr,kr->hqk', qp, past_p)).astype(jnp.float32) * SM_SCALE
        s_new = (jnp.einsum('qhc,kc->hqk', ql, cn)
                 + jnp.einsum('qhr,kr->hqk', qp, pn)).astype(jnp.float32) * SM_SCALE
        s_new = jnp.where(causal[None], s_new, NEG_INF)
        p = jax.nn.softmax(jnp.concatenate([s_past, s_new], axis=-1), axis=-1).astype(ql.dtype)
        n_past = past_c.shape[0]
        return (jnp.einsum('hqk,kc->qhc', p[..., :n_past], past_c)
                + jnp.einsum('hqk,kc->qhc', p[..., n_past:], cn))

    return lax.map(one, (q_lat, q_pe, ckv_new, kpe_new, page_table))


def mixer_ab(h, start, pool_buf, attend, w_in0, w_pool, pool_scale, q_norm, kv_norm,
             w_uq, w_uk, w_uv, w_out0):
    b, t, _ = h.shape
    z = h @ w_in0
    u_pool, c_q, c_kv, k_pe = jnp.split(
        z, [POOL_WIDTH, POOL_WIDTH + Q_LORA, POOL_WIDTH + Q_LORA + KV_LORA], axis=-1)
    y_pool, new_buf = pool_mix(u_pool, pool_buf, start, w_pool, pool_scale)
    pos = start + jnp.arange(t)
    q = (rms_norm(c_q, q_norm) @ w_uq).reshape(b, t, MLA_HEADS, QK_NOPE + QK_ROPE)
    q_lat = jnp.einsum('bthd,chd->bthc', q[..., :QK_NOPE], w_uk)
    q_pe = rope(q[..., QK_NOPE:], pos)
    ckv = rms_norm(c_kv, kv_norm)
    kpe = rope(k_pe, pos)
    o_lat = attend(q_lat, q_pe, ckv, kpe)
    o = jnp.einsum('bthc,chd->bthd', o_lat, w_uv).reshape(b, t, MLA_HEADS * V_HEAD)
    y = jnp.concatenate([y_pool, o], axis=-1) @ w_out0
    return y, new_buf, ckv, kpe


def s5_discretize(lam_re, lam_im, log_dt, b_re, b_im):
    dt = jnp.exp(log_dt.astype(jnp.float32))[:, None]
    lr = lam_re.astype(jnp.float32)
    li = lam_im.astype(jnp.float32)
    mag = jnp.exp(lr * dt)
    a_re = mag * jnp.cos(li * dt)
    a_im = mag * jnp.sin(li * dt)
    den = lr * lr + li * li
    nr, ni = a_re - 1.0, a_im
    f_re = (nr * lr + ni * li) / den
    f_im = (ni * lr - nr * li) / den
    br = b_re.astype(jnp.float32)
    bi = b_im.astype(jnp.float32)
    bb_re = f_re[..., None] * br - f_im[..., None] * bi
    bb_im = f_re[..., None] * bi + f_im[..., None] * br
    return a_re, a_im, bb_re, bb_im


def s5_combine(e1, e2):
    a1r, a1i, b1r, b1i = e1
    a2r, a2i, b2r, b2i = e2
    return (a1r * a2r - a1i * a2i, a1r * a2i + a1i * a2r,
            a2r * b1r - a2i * b1i + b2r, a2r * b1i + a2i * b1r + b2i)


def s5_layer(u, h0_re, h0_im, lam_re, lam_im, log_dt, b_re, b_im, c_re, c_im, d_skip):
    b, t, _ = u.shape
    uf = u.astype(jnp.float32)
    ug = uf.reshape(b, t, SSM_GROUPS, SSM_GROUP)
    a_re, a_im, bb_re, bb_im = s5_discretize(lam_re, lam_im, log_dt, b_re, b_im)
    bu_re = jnp.einsum('btgk,gnk->btgn', ug, bb_re)
    bu_im = jnp.einsum('btgk,gnk->btgn', ug, bb_im)
    h0r = h0_re.astype(jnp.float32)
    h0i = h0_im.astype(jnp.float32)
    bu_re = bu_re.at[:, 0].add(a_re * h0r - a_im * h0i)
    bu_im = bu_im.at[:, 0].add(a_re * h0i + a_im * h0r)
    ar = jnp.broadcast_to(a_re, bu_re.shape)
    ai = jnp.broadcast_to(a_im, bu_im.shape)
    _, _, h_re, h_im = lax.associative_scan(s5_combine, (ar, ai, bu_re, bu_im), axis=1)
    y = (jnp.einsum('btgn,gkn->btgk', h_re, c_re.astype(jnp.float32))
         - jnp.einsum('btgn,gkn->btgk', h_im, c_im.astype(jnp.float32)))
    y = y.reshape(b, t, SSM_WIDTH) + d_skip.astype(jnp.float32) * uf
    return y.astype(u.dtype), h_re[:, -1], h_im[:, -1]


def mixer_c(h, h0_re, h0_im, w_in1, lam_re, lam_im, log_dt, b_re, b_im, c_re, c_im, d_skip,
            w_glu, w_out1):
    u = h @ w_in1
    y, s_re, s_im = s5_layer(u, h0_re, h0_im, lam_re, lam_im, log_dt, b_re, b_im, c_re, c_im, d_skip)
    z = jax.nn.gelu(y)
    z = z * jax.nn.sigmoid(z @ w_glu)
    return z @ w_out1, s_re, s_im


def trunk(x, start, pool_buf, attend, h0_re, h0_im, norm_gains, ffn_w, ab_w, c_w):
    w_gate, w_up, w_down = ffn_w
    for layer in range(DEPTH):
        g = norm_gains[layer]
        h = rms_norm(x, g[0])
        x = x + 0.5 * rms_norm(swiglu(h, w_gate[layer, 0], w_up[layer, 0], w_down[layer, 0]), g[1])
        h = rms_norm(x, g[2])
        if layer % 2 == 0:
            y, pool_new, ckv_new, kpe_new = mixer_ab(h, start, pool_buf, attend, *ab_w)
        else:
            y, ssm_re, ssm_im = mixer_c(h, h0_re, h0_im, *c_w)
        x = x + rms_norm(y, g[3])
        h = rms_norm(x, g[4])
        x = x + 0.5 * rms_norm(swiglu(h, w_gate[layer, 1], w_up[layer, 1], w_down[layer, 1]), g[5])
    return x, pool_new, ckv_new, kpe_new, ssm_re, ssm_im


def setup_inputs(seed: int = 0) -> dict:
    key = jax.random.key(seed)
    ks = iter(jax.random.split(key, 40))

    def nrm(shape, scale):
        return jax.random.normal(next(ks), shape, jnp.float32) * scale

    n_pages = PAST_LEN // PAGE_SIZE
    n_pool = (DEC_BATCH * n_pages * 5) // 4
    page_table = jax.random.permutation(next(ks), n_pool)[: DEC_BATCH * n_pages]
    page_table = page_table.reshape(DEC_BATCH, n_pages).astype(jnp.int32)
    lam_re = -0.5 + nrm((SSM_GROUPS, SSM_STATE), 0.01)
    lam_im = math.pi * jnp.arange(SSM_STATE, dtype=jnp.float32)[None, :] + nrm((SSM_GROUPS, SSM_STATE), 0.01)
    log_dt = jax.random.uniform(next(ks), (SSM_GROUPS,), jnp.float32,
                                minval=math.log(DT_MIN), maxval=math.log(DT_MAX))
    return {
        "x_prompt": nrm((BATCH, SEQ, D_MODEL), 1.0),
        "x_sample": nrm((DEC_BATCH, DEC_SEQ, D_MODEL), 1.0),
        "cache_ckv": nrm((n_pool, PAGE_SIZE, KV_LORA), 1.0),
        "cache_kpe": nrm((n_pool, PAGE_SIZE, QK_ROPE), 1.0),
        "page_table": page_table,
        "state_pool": nrm((DEC_BATCH, POOL_BUF, POOL_WIDTH), 1.0),
        "state_ssm_re": nrm((DEC_BATCH, SSM_GROUPS, SSM_STATE), 0.1),
        "state_ssm_im": nrm((DEC_BATCH, SSM_GROUPS, SSM_STATE), 0.1),
        "norm_gains": 1.0 + nrm((DEPTH, N_NORMS, D_MODEL), 0.02),
        "w_ffn_gate": nrm((DEPTH, 2, D_MODEL, D_FF), D_MODEL ** -0.5),
        "w_ffn_up": nrm((DEPTH, 2, D_MODEL, D_FF), D_MODEL ** -0.5),
        "w_ffn_down": nrm((DEPTH, 2, D_FF, D_MODEL), D_FF ** -0.5),
        "w_in0": nrm((D_MODEL, MIX_IN0), D_MODEL ** -0.5),
        "w_pool": nrm((len(POOL_WINDOWS), POOL_GROUP, POOL_GROUP), POOL_GROUP ** -0.5),
        "pool_scale": 1.0 + nrm((POOL_WIDTH,), 0.1),
        "q_norm": 1.0 + nrm((Q_LORA,), 0.02),
        "kv_norm": 1.0 + nrm((KV_LORA,), 0.02),
        "w_uq": nrm((Q_LORA, MLA_HEADS * (QK_NOPE + QK_ROPE)), Q_LORA ** -0.5),
        "w_uk": nrm((KV_LORA, MLA_HEADS, QK_NOPE), KV_LORA ** -0.5),
        "w_uv": nrm((KV_LORA, MLA_HEADS, V_HEAD), KV_LORA ** -0.5),
        "w_out0": nrm((MIX_OUT0, D_MODEL), MIX_OUT0 ** -0.5),
        "w_in1": nrm((D_MODEL, SSM_WIDTH), D_MODEL ** -0.5),
        "lam_re": lam_re,
        "lam_im": lam_im,
        "log_dt": log_dt,
        "b_re": nrm((SSM_GROUPS, SSM_STATE, SSM_GROUP), (2 * SSM_GROUP) ** -0.5),
        "b_im": nrm((SSM_GROUPS, SSM_STATE, SSM_GROUP), (2 * SSM_GROUP) ** -0.5),
        "c_re": nrm((SSM_GROUPS, SSM_GROUP, SSM_STATE), (2 * SSM_STATE) ** -0.5),
        "c_im": nrm((SSM_GROUPS, SSM_GROUP, SSM_STATE), (2 * SSM_STATE) ** -0.5),
        "d_skip": nrm((SSM_WIDTH,), 1.0),
        "w_glu": nrm((SSM_WIDTH, SSM_WIDTH), SSM_WIDTH ** -0.5),
        "w_out1": nrm((SSM_WIDTH, D_MODEL), SSM_WIDTH ** -0.5),
    }


def reference(x_prompt, x_sample, cache_ckv, cache_kpe, page_table, state_pool, state_ssm_re,
              state_ssm_im, norm_gains, w_ffn_gate, w_ffn_up, w_ffn_down, w_in0, w_pool, pool_scale,
              q_norm, kv_norm, w_uq, w_uk, w_uv, w_out0, w_in1, lam_re, lam_im, log_dt, b_re, b_im,
              c_re, c_im, d_skip, w_glu, w_out1):
    ffn_w = (w_ffn_gate, w_ffn_up, w_ffn_down)
    ab_w = (w_in0, w_pool, pool_scale, q_norm, kv_norm, w_uq, w_uk, w_uv, w_out0)
    c_w = (w_in1, lam_re, lam_im, log_dt, b_re, b_im, c_re, c_im, d_skip, w_glu, w_out1)

    b_p = x_prompt.shape[0]
    zero_pool = jnp.zeros((b_p, POOL_BUF, POOL_WIDTH), x_prompt.dtype)
    zero_h = jnp.zeros((b_p, SSM_GROUPS, SSM_STATE), jnp.float32)
    y_prompt, pool_p, ckv_p, kpe_p, ssm_re_p, ssm_im_p = trunk(
        x_prompt, 0, zero_pool, mla_prompt_attn, zero_h, zero_h, norm_gains, ffn_w, ab_w, c_w)

    past_len = page_table.shape[1] * cache_ckv.shape[1]

    def attend_sample(q_lat, q_pe, ckv, kpe):
        return mla_sample_attn(q_lat, q_pe, ckv, kpe, cache_ckv, cache_kpe, page_table)

    y_sample, pool_s, ckv_s, kpe_s, ssm_re_s, ssm_im_s = trunk(
        x_sample, past_len, state_pool, attend_sample, state_ssm_re, state_ssm_im,
        norm_gains, ffn_w, ab_w, c_w)

    return (y_prompt, y_sample, pool_p, pool_s, ckv_p, ckv_s, kpe_p, kpe_s,
            ssm_re_p, ssm_im_p, ssm_re_s, ssm_im_s)
```

```python
import functools

import jax
import jax.numpy as jnp
from jax import lax
from jax.experimental import pallas as pl
from jax.experimental.pallas import tpu as pltpu

F32 = jnp.float32
BF16 = jnp.bfloat16

EPS = 1e-6
ROPE_THETA = 10000.0
POOL_WINDOWS = (2, 4, 8, 16)
NEG_INF = -1e30

V7X_VMEM_LIMIT_BYTES = 56 * 1024 * 1024
LANES = 128
SUBLANES = 8
S5_CHUNK_GROUPS = 16


def _tile(n, prefs):
    for p in prefs:
        if n % p == 0:
            return p
    return n


def _cp(*sem):
    return pltpu.CompilerParams(dimension_semantics=sem, vmem_limit_bytes=V7X_VMEM_LIMIT_BYTES)


def _rms(x, g):
    return x * lax.rsqrt(jnp.mean(x * x, axis=-1, keepdims=True) + EPS) * g


def _dot(a, b):
    return jnp.dot(a, b, preferred_element_type=F32)


def _dot_nt(a, b):
    return lax.dot_general(a, b, (((1,), (1,)), ((), ())), preferred_element_type=F32)


def _rep(x, n):
    return x if n == 1 else jnp.concatenate([x] * n, axis=1)


def _prenorm_kernel(x_ref, g_ref, o_ref):
    o_ref[...] = _rms(x_ref[...], g_ref[...]).astype(o_ref.dtype)


def _prenorm(x, g):
    t, d = x.shape
    tm = _tile(t, (416, 208, 128, 64, 32, 16))
    return pl.pallas_call(
        _prenorm_kernel,
        grid=(t // tm,),
        in_specs=[pl.BlockSpec((tm, d), lambda i: (i, 0)), pl.BlockSpec((1, d), lambda i: (0, 0))],
        out_specs=pl.BlockSpec((tm, d), lambda i: (i, 0)),
        out_shape=jax.ShapeDtypeStruct((t, d), BF16),
        compiler_params=_cp("parallel"),
        name="prenorm",
    )(x, g.reshape(1, d))


def _post_kernel(x_ref, y_ref, gp_ref, gn_ref, xo_ref, ho_ref, *, scale):
    xn = x_ref[...] + scale * _rms(y_ref[...], gp_ref[...])
    xo_ref[...] = xn
    ho_ref[...] = _rms(xn, gn_ref[...]).astype(ho_ref.dtype)


def _post_last_kernel(x_ref, y_ref, gp_ref, xo_ref, *, scale):
    xo_ref[...] = x_ref[...] + scale * _rms(y_ref[...], gp_ref[...])


def _post(x, y, g_post, scale, g_next):
    t, d = x.shape
    tm = _tile(t, (208, 128, 64, 32, 16))
    row = pl.BlockSpec((tm, d), lambda i: (i, 0))
    vec = pl.BlockSpec((1, d), lambda i: (0, 0))
    if g_next is None:
        return pl.pallas_call(
            functools.partial(_post_last_kernel, scale=scale),
            grid=(t // tm,),
            in_specs=[row, row, vec],
            out_specs=row,
            out_shape=jax.ShapeDtypeStruct((t, d), F32),
            compiler_params=_cp("parallel"),
            name="post_last",
        )(x, y, g_post.reshape(1, d)), None
    return pl.pallas_call(
        functools.partial(_post_kernel, scale=scale),
        grid=(t // tm,),
        in_specs=[row, row, vec, vec],
        out_specs=[row, row],
        out_shape=[jax.ShapeDtypeStruct((t, d), F32), jax.ShapeDtypeStruct((t, d), BF16)],
        compiler_params=_cp("parallel"),
        name="post",
    )(x, y, g_post.reshape(1, d), g_next.reshape(1, d))


def _ffn_kernel(h_ref, wg_ref, wu_ref, wd_ref, o_ref, *, n_chunks):
    j = pl.program_id(1)

    @pl.when(j == 0)
    def _():
        o_ref[...] = jnp.zeros(o_ref.shape, o_ref.dtype)

    h = h_ref[...]
    g = _dot(h, wg_ref[...])
    u = _dot(h, wu_ref[...])
    a = (g * jax.nn.sigmoid(g) * u).astype(BF16)
    cw = o_ref.shape[1] // n_chunks
    for c in range(n_chunks):
        o_ref[:, c * cw:(c + 1) * cw] += _dot(a, wd_ref[:, c * cw:(c + 1) * cw])


def _ffn(h, wg, wu, wd, layer, half):
    t, d = h.shape
    f = wg.shape[-1]
    tm = _tile(t, (1040, 512, 256, 128, 64, 32, 16))
    tf = _tile(f, (256, 128))
    n_chunks = max(1, d // 512)
    return pl.pallas_call(
        functools.partial(_ffn_kernel, n_chunks=n_chunks),
        grid=(t // tm, f // tf),
        in_specs=[
            pl.BlockSpec((tm, d), lambda i, j: (i, 0), pipeline_mode=pl.Buffered(1)),
            pl.BlockSpec((None, None, d, tf), lambda i, j: (layer, half, 0, j)),
            pl.BlockSpec((None, None, d, tf), lambda i, j: (layer, half, 0, j)),
            pl.BlockSpec((None, None, tf, d), lambda i, j: (layer, half, j, 0)),
        ],
        out_specs=pl.BlockSpec((tm, d), lambda i, j: (i, 0), pipeline_mode=pl.Buffered(1)),
        out_shape=jax.ShapeDtypeStruct((t, d), F32),
        compiler_params=_cp("parallel", "arbitrary"),
        name="ffn",
    )(h, wg, wu, wd)


def _mm_kernel(x_ref, w_ref, o_ref):
    o_ref[...] = _dot(x_ref[...], w_ref[...]).astype(o_ref.dtype)


def _mm_specs(t, k, n):
    tm = _tile(t, (1040, 512, 256, 128, 64, 32, 16))
    tn = _tile(n, (512, 256, 128))
    return tm, tn


def _mm(x, w, out_dtype=F32):
    t, k = x.shape
    n = w.shape[1]
    tm, tn = _mm_specs(t, k, n)
    return pl.pallas_call(
        _mm_kernel,
        grid=(t // tm, n // tn),
        in_specs=[pl.BlockSpec((tm, k), lambda i, j: (i, 0)), pl.BlockSpec((k, tn), lambda i, j: (0, j))],
        out_specs=pl.BlockSpec((tm, tn), lambda i, j: (i, j)),
        out_shape=jax.ShapeDtypeStruct((t, n), out_dtype),
        compiler_params=_cp("parallel", "arbitrary"),
        name="mm",
    )(x, w)


def _mm2_kernel(x1_ref, x2_ref, w_ref, o_ref):
    k1 = x1_ref.shape[1]
    o_ref[...] = _dot(x1_ref[...], w_ref[:k1, :]) + _dot(x2_ref[...], w_ref[k1:, :])


def _mm2(x1, x2, w):
    t, k1 = x1.shape
    k2 = x2.shape[1]
    n = w.shape[1]
    tm, tn = _mm_specs(t, k1 + k2, n)
    return pl.pallas_call(
        _mm2_kernel,
        grid=(t // tm, n // tn),
        in_specs=[pl.BlockSpec((tm, k1), lambda i, j: (i, 0)), pl.BlockSpec((tm, k2), lambda i, j: (i, 0)),
                  pl.BlockSpec((k1 + k2, tn), lambda i, j: (0, j))],
        out_specs=pl.BlockSpec((tm, tn), lambda i, j: (i, j)),
        out_shape=jax.ShapeDtypeStruct((t, n), F32),
        compiler_params=_cp("parallel", "arbitrary"),
        name="mm2",
    )(x1, x2, w)


def _mm_glu_kernel(zb_ref, w_ref, z_ref, o_ref):
    s = _dot(zb_ref[...], w_ref[...])
    o_ref[...] = (z_ref[...] * jax.nn.sigmoid(s)).astype(o_ref.dtype)


def _mm_glu(zb, w, z):
    t, k = zb.shape
    n = w.shape[1]
    tm, tn = _mm_specs(t, k, n)
    return pl.pallas_call(
        _mm_glu_kernel,
        grid=(t // tm, n // tn),
        in_specs=[pl.BlockSpec((tm, k), lambda i, j: (i, 0)), pl.BlockSpec((k, tn), lambda i, j: (0, j)),
                  pl.BlockSpec((tm, tn), lambda i, j: (i, j))],
        out_specs=pl.BlockSpec((tm, tn), lambda i, j: (i, j)),
        out_shape=jax.ShapeDtypeStruct((t, n), BF16),
        compiler_params=_cp("parallel", "arbitrary"),
        name="mm_glu",
    )(zb, w, z)


def _pool_prompt_kernel(u_ref, wp_ref, ps_ref, o_ref):
    s_len = u_ref.shape[0]
    pg = wp_ref.shape[1]
    row = lax.broadcasted_iota(jnp.int32, (s_len, pg), 0)
    for gi, w in enumerate(POOL_WINDOWS):
        cols = slice(gi * pg, (gi + 1) * pg)
        u = u_ref[:, cols]
        s = u
        d = 1
        while d < w:
            s = s + jnp.where(row >= d, pltpu.roll(s, d, axis=0), 0.0)
            d *= 2
        cnt = jnp.minimum(row + 1, w).astype(F32)
        dd = (s / cnt - u).astype(BF16)
        o_ref[:, cols] = (_dot(dd, wp_ref[gi]) * ps_ref[:, cols]).astype(o_ref.dtype)


def _pool_prompt(u_pool, w_pool, pool_scale, batch, seq):
    c = u_pool.shape[1]
    return pl.pallas_call(
        _pool_prompt_kernel,
        grid=(batch,),
        in_specs=[pl.BlockSpec((seq, c), lambda b: (b, 0)),
                  pl.BlockSpec(w_pool.shape, lambda b: (0, 0, 0)),
                  pl.BlockSpec((1, c), lambda b: (0, 0))],
        out_specs=pl.BlockSpec((seq, c), lambda b: (b, 0)),
        out_shape=jax.ShapeDtypeStruct((batch * seq, c), BF16),
        compiler_params=_cp("parallel"),
        name="pool_prompt",
    )(u_pool, w_pool, pool_scale.reshape(1, c))


def _pool_sample_kernel(ext_ref, wp_ref, ps_ref, o_ref, *, start):
    nb = ext_ref.shape[0]
    pg = wp_ref.shape[1]
    for gi, w in enumerate(POOL_WINDOWS):
        cols = slice(gi * pg, (gi + 1) * pg)
        u = ext_ref[nb - 1, :, cols]
        s = u
        for i in range(1, w):
            s = s + ext_ref[nb - 1 - i, :, cols]
        dd = (s / float(min(start + 1, w)) - u).astype(BF16)
        o_ref[:, cols] = (_dot(dd, wp_ref[gi]) * ps_ref[:, cols]).astype(o_ref.dtype)


def _pool_sample(ext_t, w_pool, pool_scale, start):
    nb, bd, c = ext_t.shape
    return pl.pallas_call(
        functools.partial(_pool_sample_kernel, start=start),
        grid=(1,),
        in_specs=[pl.BlockSpec((nb, bd, c), lambda i: (0, 0, 0)),
                  pl.BlockSpec(w_pool.shape, lambda i: (0, 0, 0)),
                  pl.BlockSpec((1, c), lambda i: (0, 0))],
        out_specs=pl.BlockSpec((bd, c), lambda i: (0, 0)),
        out_shape=jax.ShapeDtypeStruct((bd, c), BF16),
        compiler_params=_cp("arbitrary"),
        name="pool_sample",
    )(ext_t, w_pool, pool_scale.reshape(1, c))


def _mla_prep_kernel(h_ref, wq_ref, wkv_ref, wkp_ref, wkps_ref, qn_ref, kvn_ref, cos_ref, sin_ref,
                     cq_ref, ckv_ref, ckvb_ref, kpe_ref, kpeb_ref):
    h = h_ref[...]
    cq_ref[...] = _rms(_dot(h, wq_ref[...]), qn_ref[...]).astype(cq_ref.dtype)
    ckv = _rms(_dot(h, wkv_ref[...]), kvn_ref[...])
    ckv_ref[...] = ckv
    ckvb_ref[...] = ckv.astype(ckvb_ref.dtype)
    kpe = _dot(h, wkp_ref[...]) * cos_ref[...] + _dot(h, wkps_ref[...]) * sin_ref[...]
    kpe_ref[...] = kpe
    kpeb_ref[...] = kpe.astype(kpeb_ref.dtype)


def _mla_prep(h, w_cq, w_ckv, w_kpe, w_kpe_sw, q_norm, kv_norm, cosf, sinf):
    t, d = h.shape
    ql, kl, r = w_cq.shape[1], w_ckv.shape[1], w_kpe.shape[1]
    tm = _tile(t, (416, 208, 128, 64, 32, 16))
    row = lambda n: pl.BlockSpec((tm, n), lambda i: (i, 0))
    full = lambda a, b: pl.BlockSpec((a, b), lambda i: (0, 0))
    return pl.pallas_call(
        _mla_prep_kernel,
        grid=(t // tm,),
        in_specs=[row(d), full(d, ql), full(d, kl), full(d, r), full(d, r), full(1, ql), full(1, kl),
                  row(r), row(r)],
        out_specs=[row(ql), row(kl), row(kl), row(r), row(r)],
        out_shape=[jax.ShapeDtypeStruct((t, ql), BF16), jax.ShapeDtypeStruct((t, kl), F32),
                   jax.ShapeDtypeStruct((t, kl), BF16), jax.ShapeDtypeStruct((t, r), F32),
                   jax.ShapeDtypeStruct((t, r), BF16)],
        compiler_params=_cp("parallel"),
        name="mla_prep",
    )(h, w_cq, w_ckv, w_kpe, w_kpe_sw, q_norm.reshape(1, ql), kv_norm.reshape(1, kl), cosf, sinf)


def _q_prep_kernel(cq_ref, wqn_ref, wqr_ref, wqrs_ref, wuk_ref, cos_ref, sin_ref, ql_ref, qp_ref):
    cq = cq_ref[...]
    q_nope = _dot(cq, wqn_ref[0]).astype(BF16)
    ql_ref[0] = _dot(q_nope, wuk_ref[0]).astype(ql_ref.dtype)
    q_pe = _dot(cq, wqr_ref[0]) * cos_ref[...] + _dot(cq, wqrs_ref[0]) * sin_ref[...]
    qp_ref[0] = q_pe.astype(qp_ref.dtype)


def _q_prep(cq, w_qn, w_qr, w_qr_sw, w_uk_t, cosf, sinf):
    t, ql = cq.shape
    heads, _, nope = w_qn.shape
    r = w_qr.shape[2]
    kl = w_uk_t.shape[2]
    tm = _tile(t, (1040, 512, 256, 128, 64, 32, 16))
    hw = lambda a, b: pl.BlockSpec((1, a, b), lambda i, h: (h, 0, 0))
    return pl.pallas_call(
        _q_prep_kernel,
        grid=(t // tm, heads),
        in_specs=[pl.BlockSpec((tm, ql), lambda i, h: (i, 0)), hw(ql, nope), hw(ql, r), hw(ql, r), hw(nope, kl),
                  pl.BlockSpec((tm, r), lambda i, h: (i, 0)), pl.BlockSpec((tm, r), lambda i, h: (i, 0))],
        out_specs=[pl.BlockSpec((1, tm, kl), lambda i, h: (h, i, 0)),
                   pl.BlockSpec((1, tm, r), lambda i, h: (h, i, 0))],
        out_shape=[jax.ShapeDtypeStruct((heads, t, kl), BF16), jax.ShapeDtypeStruct((heads, t, r), BF16)],
        compiler_params=_cp("parallel", "arbitrary"),
        name="q_prep",
    )(cq, w_qn, w_qr, w_qr_sw, w_uk_t, cosf, sinf)


def _o_proj_kernel(ol_ref, wuv_ref, o_ref):
    o_ref[...] = _dot(ol_ref[0], wuv_ref[0]).astype(o_ref.dtype)


def _o_proj(o_lat, w_uv_t):
    heads, t, kl = o_lat.shape
    vh = w_uv_t.shape[2]
    tm = _tile(t, (1024, 512, 256, 128, 64, 32, 16))
    return pl.pallas_call(
        _o_proj_kernel,
        grid=(t // tm, heads),
        in_specs=[pl.BlockSpec((1, tm, kl), lambda i, h: (h, i, 0)), pl.BlockSpec((1, kl, vh), lambda i, h: (h, 0, 0))],
        out_specs=pl.BlockSpec((tm, vh), lambda i, h: (i, h)),
        out_shape=jax.ShapeDtypeStruct((t, heads * vh), BF16),
        compiler_params=_cp("parallel", "arbitrary"),
        name="o_proj",
    )(o_lat, w_uv_t)


def _softmax_update(s, v, m_ref, l_ref, acc_ref):
    m_prev = m_ref[...]
    m_new = jnp.maximum(m_prev, jnp.max(s, axis=1, keepdims=True))
    alpha = jnp.exp(m_prev - m_new)
    p = jnp.exp(s - _rep(m_new, s.shape[1] // LANES))
    l_ref[...] = alpha * l_ref[...] + jnp.sum(p, axis=1, keepdims=True)
    acc_ref[...] = acc_ref[...] * _rep(alpha, acc_ref.shape[1] // LANES) + _dot(p.astype(BF16), v)
    m_ref[...] = m_new


def _attn_prompt_kernel(ql_ref, qp_ref, k_ref, kp_ref, o_ref, m_ref, l_ref, acc_ref, *, scale):
    qi = pl.program_id(1)
    hg, tq, c = ql_ref.shape
    r = hg * tq
    q = ql_ref[...].reshape(r, c)
    qp = qp_ref[...].reshape(r, qp_ref.shape[2])
    m_ref[...] = jnp.full(m_ref.shape, NEG_INF, F32)
    l_ref[...] = jnp.zeros(l_ref.shape, F32)
    acc_ref[...] = jnp.zeros(acc_ref.shape, F32)

    def block(kb, masked):
        start = pl.multiple_of(kb * tq, tq)
        k = k_ref[pl.ds(start, tq), :]
        kp = kp_ref[pl.ds(start, tq), :]
        s = (_dot_nt(q, k) + _dot_nt(qp, kp)) * scale
        if masked:
            s3 = s.reshape(hg, tq, tq)
            qpos = lax.broadcasted_iota(jnp.int32, s3.shape, 1)
            kpos = lax.broadcasted_iota(jnp.int32, s3.shape, 2)
            s = jnp.where(kpos <= qpos, s3, NEG_INF).reshape(r, tq)
        _softmax_update(s, k, m_ref, l_ref, acc_ref)

    def body(kb, carry):
        block(kb, False)
        return carry

    lax.fori_loop(0, qi, body, 0)
    block(qi, True)
    o = acc_ref[...] / _rep(l_ref[...], c // LANES)
    o_ref[...] = o.astype(o_ref.dtype).reshape(hg, tq, c)


def _attn_prompt(q_lat, q_pe, ckv_b, kpe_b, batch, seq, scale):
    heads, _, c = q_lat.shape
    r = q_pe.shape[2]
    tq = _tile(seq, (256, 128))
    hg = _tile(heads, (12, 8, 4, 2, 1))
    nq = seq // tq
    rows = hg * tq
    return pl.pallas_call(
        functools.partial(_attn_prompt_kernel, scale=scale),
        grid=(batch, nq, heads // hg),
        in_specs=[pl.BlockSpec((hg, tq, c), lambda b, i, g: (g, b * nq + i, 0)),
                  pl.BlockSpec((hg, tq, r), lambda b, i, g: (g, b * nq + i, 0)),
                  pl.BlockSpec((seq, c), lambda b, i, g: (b, 0)),
                  pl.BlockSpec((seq, r), lambda b, i, g: (b, 0))],
        out_specs=pl.BlockSpec((hg, tq, c), lambda b, i, g: (g, b * nq + i, 0)),
        out_shape=jax.ShapeDtypeStruct((heads, batch * seq, c), BF16),
        scratch_shapes=[pltpu.VMEM((rows, LANES), F32), pltpu.VMEM((rows, LANES), F32),
                        pltpu.VMEM((rows, c), F32)],
        compiler_params=_cp("parallel", "parallel", "arbitrary"),
        name="attn_prompt",
    )(q_lat, q_pe, ckv_b, kpe_b)


def _attn_sample_kernel(pt_ref, ql_ref, qp_ref, cn_ref, pn_ref, *rest, pps, scale):
    del pt_ref
    ck_refs, kp_refs = rest[:pps], rest[pps:2 * pps]
    o_ref, m_ref, l_ref, acc_ref = rest[2 * pps:]
    step = pl.program_id(1)
    q = ql_ref[0]
    qp = qp_ref[0]

    @pl.when(step == 0)
    def _():
        cn = cn_ref[0]
        pn = pn_ref[0]
        s0 = (jnp.sum(q.astype(F32) * cn, axis=1, keepdims=True)
              + jnp.sum(qp.astype(F32) * pn, axis=1, keepdims=True)) * scale
        m_ref[...] = jnp.broadcast_to(s0, m_ref.shape)
        l_ref[...] = jnp.ones(l_ref.shape, F32)
        acc_ref[...] = jnp.broadcast_to(cn, acc_ref.shape)

    k = jnp.concatenate([ref[0].astype(BF16) for ref in ck_refs], axis=0)
    kp = jnp.concatenate([ref[0].astype(BF16) for ref in kp_refs], axis=0)
    s = (_dot_nt(q, k) + _dot_nt(qp, kp)) * scale
    _softmax_update(s, k, m_ref, l_ref, acc_ref)

    @pl.when(step == pl.num_programs(1) - 1)
    def _():
        o_ref[0] = (acc_ref[...] / _rep(l_ref[...], acc_ref.shape[1] // LANES)).astype(o_ref.dtype)


def _attn_sample(q_lat, q_pe, ckv_new, kpe_new, cache_ckv, cache_kpe, page_table, scale):
    bd, heads, c = q_lat.shape
    r = q_pe.shape[2]
    n_pages = page_table.shape[1]
    page = cache_ckv.shape[1]
    pps = _tile(n_pages, (16, 8, 4, 2, 1))

    def page_map(i, b, s, pt):
        return (pt[b * n_pages + s * pps + i], 0, 0)

    per_seq = lambda n, w: pl.BlockSpec((1, n, w), lambda b, s, pt: (b, 0, 0))
    in_specs = [per_seq(heads, c), per_seq(heads, r), per_seq(1, c), per_seq(1, r)]
    in_specs += [pl.BlockSpec((1, page, c), functools.partial(page_map, i)) for i in range(pps)]
    in_specs += [pl.BlockSpec((1, page, r), functools.partial(page_map, i)) for i in range(pps)]
    return pl.pallas_call(
        functools.partial(_attn_sample_kernel, pps=pps, scale=scale),
        grid_spec=pltpu.PrefetchScalarGridSpec(
            num_scalar_prefetch=1,
            grid=(bd, n_pages // pps),
            in_specs=in_specs,
            out_specs=per_seq(heads, c),
            scratch_shapes=[pltpu.VMEM((heads, LANES), F32), pltpu.VMEM((heads, LANES), F32),
                            pltpu.VMEM((heads, c), F32)],
        ),
        out_shape=jax.ShapeDtypeStruct((bd, heads, c), BF16),
        compiler_params=_cp("parallel", "arbitrary"),
        name="attn_sample",
    )(page_table.reshape(-1), q_lat, q_pe, ckv_new, kpe_new, *([cache_ckv] * pps), *([cache_kpe] * pps))


def _s5_disc_kernel(lr_ref, li_ref, ldt_ref, br_ref, bi_ref, bbr_ref, bbi_ref, pr_ref, pi_ref):
    dt = jnp.exp(ldt_ref[...])
    lr = lr_ref[...]
    li = li_ref[...]
    mag = jnp.exp(lr * dt)
    a_re = mag * jnp.cos(li * dt)
    a_im = mag * jnp.sin(li * dt)
    den = lr * lr + li * li
    nr, ni = a_re - 1.0, a_im
    f_re = (nr * lr + ni * li) / den
    f_im = (ni * lr - nr * li) / den
    for k in range(br_ref.shape[0]):
        bbr_ref[k] = f_re * br_ref[k] - f_im * bi_ref[k]
        bbi_ref[k] = f_re * bi_ref[k] + f_im * br_ref[k]
    p_re, p_im = a_re, a_im
    for s in range(pr_ref.shape[0]):
        pr_ref[s] = p_re
        pi_ref[s] = p_im
        p_re, p_im = p_re * a_re - p_im * a_im, p_re * a_im + p_im * a_re


def _s5_disc(lam_re, lam_im, log_dt, b_re_t, b_im_t):
    k, g, n = b_re_t.shape
    spec2 = pl.BlockSpec((g, n), lambda i: (0, 0))
    spec3 = lambda a: pl.BlockSpec((a, g, n), lambda i: (0, 0, 0))
    return pl.pallas_call(
        _s5_disc_kernel,
        grid=(1,),
        in_specs=[spec2, spec2, spec2, spec3(k), spec3(k)],
        out_specs=[spec3(k), spec3(k), spec3(SUBLANES), spec3(SUBLANES)],
        out_shape=[jax.ShapeDtypeStruct((k, g, n), F32)] * 2 + [jax.ShapeDtypeStruct((SUBLANES, g, n), F32)] * 2,
        compiler_params=_cp("arbitrary"),
        name="s5_disc",
    )(lam_re, lam_im, jnp.broadcast_to(log_dt[:, None], (g, n)), b_re_t, b_im_t)


def _s5_tail(h_re, h_im, u, cr_ref, ci_ref, d_ref, z_ref, zb_ref):
    y = _dot(h_re.astype(BF16), cr_ref[0]) - _dot(h_im.astype(BF16), ci_ref[0]) + d_ref[...] * u
    z = jax.nn.gelu(y, approximate=True)
    z_ref[...] = z
    zb_ref[...] = z.astype(zb_ref.dtype)


def _s5_scan_kernel(u_ref, bbr_ref, bbi_ref, pr_ref, pi_ref, cr_ref, ci_ref, d_ref, h0r_ref, h0i_ref,
                    z_ref, zb_ref, hr_out_ref, hi_out_ref, sr_ref, si_ref, cr_carry, ci_carry):
    tc = pl.program_id(2)
    tt, w = sr_ref.shape

    @pl.when(tc == 0)
    def _():
        cr_carry[...] = h0r_ref[0, 0]
        ci_carry[...] = h0i_ref[0, 0]

    u = u_ref[...]
    ub = u.astype(BF16)
    sr_ref[...] = _dot(ub, bbr_ref[0])
    si_ref[...] = _dot(ub, bbi_ref[0])
    p_re = pr_ref[0]
    p_im = pi_ref[0]
    sub = lax.broadcasted_iota(jnp.int32, (SUBLANES, w), 0)
    steps = []
    d = 1
    while d < SUBLANES:
        steps.append((d, jnp.where(sub >= d, p_re[d - 1:d, :], 0.0), jnp.where(sub >= d, p_im[d - 1:d, :], 0.0)))
        d *= 2

    def tile(i, carry):
        c_re, c_im = carry
        off = pl.multiple_of(i * SUBLANES, SUBLANES)
        x_re = sr_ref[pl.ds(off, SUBLANES), :]
        x_im = si_ref[pl.ds(off, SUBLANES), :]
        for d, ad_re, ad_im in steps:
            s_re = pltpu.roll(x_re, d, axis=0)
            s_im = pltpu.roll(x_im, d, axis=0)
            x_re, x_im = x_re + ad_re * s_re - ad_im * s_im, x_im + ad_re * s_im + ad_im * s_re
        x_re, x_im = x_re + p_re * c_re - p_im * c_im, x_im + p_re * c_im + p_im * c_re
        sr_ref[pl.ds(off, SUBLANES), :] = x_re
        si_ref[pl.ds(off, SUBLANES), :] = x_im
        return x_re[SUBLANES - 1:, :], x_im[SUBLANES - 1:, :]

    c_re, c_im = lax.fori_loop(0, tt // SUBLANES, tile, (cr_carry[...], ci_carry[...]))
    cr_carry[...] = c_re
    ci_carry[...] = c_im
    _s5_tail(sr_ref[...], si_ref[...], u, cr_ref, ci_ref, d_ref, z_ref, zb_ref)

    @pl.when(tc == pl.num_programs(2) - 1)
    def _():
        hr_out_ref[0, 0] = c_re
        hi_out_ref[0, 0] = c_im


def _s5_scan(u, row_off, batch, seq, bb_re, bb_im, pw_re, pw_im, cm_re, cm_im, d_skip, h0_re, h0_im):
    n_chunks, cw_in, cw_st = bb_re.shape
    tt = _tile(seq, (512, 256, 128, 64, 32, 16, 8))
    nt = seq // tt
    ob = row_off // tt
    cmat = lambda a, b: pl.BlockSpec((1, a, b), lambda b_, c, t: (c, 0, 0))
    st = pl.BlockSpec((1, 1, 1, cw_st), lambda b_, c, t: (b_, c, 0, 0))
    urow = pl.BlockSpec((tt, cw_in), lambda b_, c, t: (ob + b_ * nt + t, c))
    orow = pl.BlockSpec((tt, cw_in), lambda b_, c, t: (b_ * nt + t, c))
    rows = batch * seq
    wd = n_chunks * cw_in
    return pl.pallas_call(
        _s5_scan_kernel,
        grid=(batch, n_chunks, nt),
        in_specs=[urow, cmat(cw_in, cw_st), cmat(cw_in, cw_st), cmat(SUBLANES, cw_st), cmat(SUBLANES, cw_st),
                  cmat(cw_st, cw_in), cmat(cw_st, cw_in), pl.BlockSpec((1, cw_in), lambda b_, c, t: (0, c)), st, st],
        out_specs=[orow, orow, st, st],
        out_shape=[jax.ShapeDtypeStruct((rows, wd), F32), jax.ShapeDtypeStruct((rows, wd), BF16),
                   jax.ShapeDtypeStruct(h0_re.shape, F32), jax.ShapeDtypeStruct(h0_im.shape, F32)],
        scratch_shapes=[pltpu.VMEM((tt, cw_st), F32), pltpu.VMEM((tt, cw_st), F32),
                        pltpu.VMEM((1, cw_st), F32), pltpu.VMEM((1, cw_st), F32)],
        compiler_params=_cp("parallel", "parallel", "arbitrary"),
        name="s5_scan",
    )(u, bb_re, bb_im, pw_re, pw_im, cm_re, cm_im, d_skip.reshape(1, wd), h0_re, h0_im)


def _s5_step_kernel(u_ref, bbr_ref, bbi_ref, pr_ref, pi_ref, cr_ref, ci_ref, d_ref, h0r_ref, h0i_ref,
                    z_ref, zb_ref, hr_out_ref, hi_out_ref):
    u = u_ref[...]
    ub = u.astype(BF16)
    a_re = pr_ref[0][0:1, :]
    a_im = pi_ref[0][0:1, :]
    h0r = h0r_ref[...]
    h0i = h0i_ref[...]
    h_re = _dot(ub, bbr_ref[0]) + (a_re * h0r - a_im * h0i)
    h_im = _dot(ub, bbi_ref[0]) + (a_re * h0i + a_im * h0r)
    hr_out_ref[...] = h_re
    hi_out_ref[...] = h_im
    _s5_tail(h_re, h_im, u, cr_ref, ci_ref, d_ref, z_ref, zb_ref)


def _s5_step(u, row_off, n_seq, bb_re, bb_im, pw_re, pw_im, cm_re, cm_im, d_skip, h0_re, h0_im):
    n_chunks, cw_in, cw_st = bb_re.shape
    ob = row_off // n_seq
    cmat = lambda a, b: pl.BlockSpec((1, a, b), lambda c: (c, 0, 0))
    st = pl.BlockSpec((n_seq, cw_st), lambda c: (0, c))
    orow = pl.BlockSpec((n_seq, cw_in), lambda c: (0, c))
    wd = n_chunks * cw_in
    return pl.pallas_call(
        _s5_step_kernel,
        grid=(n_chunks,),
        in_specs=[pl.BlockSpec((n_seq, cw_in), lambda c: (ob, c)), cmat(cw_in, cw_st), cmat(cw_in, cw_st),
                  cmat(SUBLANES, cw_st), cmat(SUBLANES, cw_st), cmat(cw_st, cw_in), cmat(cw_st, cw_in),
                  pl.BlockSpec((1, cw_in), lambda c: (0, c)), st, st],
        out_specs=[orow, orow, st, st],
        out_shape=[jax.ShapeDtypeStruct((n_seq, wd), F32), jax.ShapeDtypeStruct((n_seq, wd), BF16),
                   jax.ShapeDtypeStruct(h0_re.shape, F32), jax.ShapeDtypeStruct(h0_im.shape, F32)],
        compiler_params=_cp("parallel"),
        name="s5_step",
    )(u, bb_re, bb_im, pw_re, pw_im, cm_re, cm_im, d_skip.reshape(1, wd), h0_re, h0_im)


def _block_diag(m, gpc):
    g, a, b = m.shape
    eye = jnp.eye(gpc, dtype=m.dtype)
    mc = m.reshape(g // gpc, gpc, a, b)
    blk = mc[:, :, :, None, :] * eye[None, :, None, :, None]
    return blk.reshape(g // gpc, gpc * a, gpc * b)


def kernel(x_prompt, x_sample, cache_ckv, cache_kpe, page_table, state_pool, state_ssm_re, state_ssm_im, norm_gains, w_ffn_gate, w_ffn_up, w_ffn_down, w_in0, w_pool, pool_scale, q_norm, kv_norm, w_uq, w_uk, w_uv, w_out0, w_in1, lam_re, lam_im, log_dt, b_re, b_im, c_re, c_im, d_skip, w_glu, w_out1):
    batch, seq, d_model = x_prompt.shape
    bd, dec_seq, _ = x_sample.shape
    assert dec_seq == 1, "the sample group carries one new token per sequence"
    n_prompt = batch * seq
    n_tok = n_prompt + bd
    pool_buf, pool_width = state_pool.shape[1], state_pool.shape[2]
    assert seq >= pool_buf and n_prompt % bd == 0
    kv_lora, heads, qk_nope = w_uk.shape
    v_head = w_uv.shape[2]
    q_lora = q_norm.shape[0]
    qk_rope = cache_kpe.shape[2]
    half = qk_rope // 2
    past_len = page_table.shape[1] * cache_ckv.shape[1]
    sm_scale = float(qk_nope + qk_rope) ** -0.5
    groups, n_state, grp = b_re.shape
    depth = norm_gains.shape[0]
    assert depth == 2

    wg, wu, wd = w_ffn_gate.astype(BF16), w_ffn_up.astype(BF16), w_ffn_down.astype(BF16)
    o_q, o_kv, o_pe = pool_width, pool_width + q_lora, pool_width + q_lora + kv_lora
    swap = jnp.concatenate([jnp.arange(half, qk_rope), jnp.arange(half)])
    w_in0b = w_in0.astype(BF16)
    w_upool, w_cq, w_ckv, w_kpe = w_in0b[:, :o_q], w_in0b[:, o_q:o_kv], w_in0b[:, o_kv:o_pe], w_in0b[:, o_pe:]
    w_kpe_sw = w_kpe[:, swap]
    w_uq3 = w_uq.astype(BF16).reshape(q_lora, heads, qk_nope + qk_rope).transpose(1, 0, 2)
    w_qn, w_qr = w_uq3[:, :, :qk_nope], w_uq3[:, :, qk_nope:]
    w_qr_sw = w_qr[:, :, swap]
    w_uk_t = w_uk.astype(BF16).transpose(1, 2, 0)
    w_uv_t = w_uv.astype(BF16).transpose(1, 0, 2)
    w_poolb, w_out0b = w_pool.astype(BF16), w_out0.astype(BF16)
    w_in1b, w_glub, w_out1b = w_in1.astype(BF16), w_glu.astype(BF16), w_out1.astype(BF16)

    inv_freq = ROPE_THETA ** (-jnp.arange(half, dtype=F32) / half)
    pos = jnp.concatenate([jnp.tile(jnp.arange(seq), batch), jnp.full((bd,), past_len)]).astype(F32)
    ang = pos[:, None] * inv_freq[None, :]
    cos, sin = jnp.cos(ang), jnp.sin(ang)
    cosf = jnp.concatenate([cos, cos], axis=1)
    sinf = jnp.concatenate([-sin, sin], axis=1)

    x = jnp.concatenate([x_prompt.reshape(n_prompt, d_model), x_sample.reshape(bd, d_model)], axis=0)

    g = norm_gains[0]
    h = _prenorm(x, g[0])
    x, h = _post(x, _ffn(h, wg, wu, wd, 0, 0), g[1], 0.5, g[2])

    u_pool = _mm(h, w_upool)
    cq, ckv, ckv_b, kpe, kpe_b = _mla_prep(h, w_cq, w_ckv, w_kpe, w_kpe_sw, q_norm, kv_norm, cosf, sinf)
    y_pool_p = _pool_prompt(u_pool, w_poolb, pool_scale, batch, seq)
    ext_t = jnp.concatenate([state_pool.transpose(1, 0, 2), u_pool[None, n_prompt:]], axis=0)
    y_pool_s = _pool_sample(ext_t, w_poolb, pool_scale, past_len)
    y_pool = jnp.concatenate([y_pool_p, y_pool_s], axis=0)

    q_lat, q_pe = _q_prep(cq, w_qn, w_qr, w_qr_sw, w_uk_t, cosf, sinf)
    o_lat_p = _attn_prompt(q_lat, q_pe, ckv_b, kpe_b, batch, seq, sm_scale)
    o_lat_s = _attn_sample(q_lat[:, n_prompt:].transpose(1, 0, 2), q_pe[:, n_prompt:].transpose(1, 0, 2),
                           ckv[n_prompt:, None, :], kpe[n_prompt:, None, :],
                           cache_ckv, cache_kpe, page_table, sm_scale)
    o = jnp.concatenate([_o_proj(o_lat_p, w_uv_t), _o_proj(o_lat_s.transpose(1, 0, 2), w_uv_t)], axis=0)
    y = _mm2(y_pool, o, w_out0b)

    x, h = _post(x, y, g[3], 1.0, g[4])
    g1 = norm_gains[1]
    x, h = _post(x, _ffn(h, wg, wu, wd, 0, 1), g[5], 0.5, g1[0])

    x, h = _post(x, _ffn(h, wg, wu, wd, 1, 0), g1[1], 0.5, g1[2])

    u = _mm(h, w_in1b)
    gpc = _tile(groups, (S5_CHUNK_GROUPS, 8, 4, 2, 1))
    n_chunks = groups // gpc
    cw_st = gpc * n_state
    bbt_re, bbt_im, pw_re, pw_im = _s5_disc(lam_re, lam_im, log_dt, b_re.transpose(2, 0, 1), b_im.transpose(2, 0, 1))
    bb_re = _block_diag(bbt_re.transpose(1, 0, 2), gpc).astype(BF16)
    bb_im = _block_diag(bbt_im.transpose(1, 0, 2), gpc).astype(BF16)
    cm_re = _block_diag(c_re.transpose(0, 2, 1), gpc).astype(BF16)
    cm_im = _block_diag(c_im.transpose(0, 2, 1), gpc).astype(BF16)
    pw_re = pw_re.reshape(SUBLANES, n_chunks, cw_st).transpose(1, 0, 2)
    pw_im = pw_im.reshape(SUBLANES, n_chunks, cw_st).transpose(1, 0, 2)
    zero_h = jnp.zeros((batch, n_chunks, 1, cw_st), F32)
    z_p, zb_p, hr_p, hi_p = _s5_scan(u, 0, batch, seq, bb_re, bb_im, pw_re, pw_im, cm_re, cm_im, d_skip,
                                     zero_h, zero_h)
    z_s, zb_s, hr_s, hi_s = _s5_step(u, n_prompt, bd, bb_re, bb_im, pw_re, pw_im, cm_re, cm_im, d_skip,
                                     state_ssm_re.reshape(bd, groups * n_state),
                                     state_ssm_im.reshape(bd, groups * n_state))
    z = jnp.concatenate([z_p, z_s], axis=0)
    zb = jnp.concatenate([zb_p, zb_s], axis=0)
    y = _mm(_mm_glu(zb, w_glub, z), w_out1b)

    x, h = _post(x, y, g1[3], 1.0, g1[4])
    x, _ = _post(x, _ffn(h, wg, wu, wd, 1, 1), g1[5], 0.5, None)

    y_prompt = x[:n_prompt].reshape(batch, seq, d_model)
    y_sample = x[n_prompt:].reshape(bd, 1, d_model)
    pool_p = u_pool[:n_prompt].reshape(batch, seq, pool_width)[:, seq - pool_buf:]
    pool_s = jnp.concatenate([state_pool[:, 1:], u_pool[n_prompt:, None, :]], axis=1)
    ckv_p = ckv[:n_prompt].reshape(batch, seq, kv_lora)
    ckv_s = ckv[n_prompt:].reshape(bd, 1, kv_lora)
    kpe_p = kpe[:n_prompt].reshape(batch, seq, qk_rope)
    kpe_s = kpe[n_prompt:].reshape(bd, 1, qk_rope)
    return (y_prompt, y_sample, pool_p, pool_s, ckv_p, ckv_s, kpe_p, kpe_s,
            hr_p.reshape(batch, groups, n_state), hi_p.reshape(batch, groups, n_state),
            hr_s.reshape(bd, groups, n_state), hi_s.reshape(bd, groups, n_state))
```

```python
import functools

import jax
import jax.numpy as jnp
from jax import lax
from jax.experimental import pallas as pl
from jax.experimental.pallas import tpu as pltpu

F32 = jnp.float32
BF16 = jnp.bfloat16

EPS = 1e-6
ROPE_THETA = 10000.0
POOL_WINDOWS = (2, 4, 8, 16)
NEG_INF = -1e30

V7X_VMEM_LIMIT_BYTES = 56 * 1024 * 1024
FFN_VMEM_LIMIT_BYTES = 58 * 1024 * 1024
LANES = 128
SUBLANES = 8
S5_CHUNK_GROUPS = 16


def _tile(n, prefs):
    for p in prefs:
        if n % p == 0:
            return p
    return n


def _cp(*sem):
    return pltpu.CompilerParams(dimension_semantics=sem, vmem_limit_bytes=V7X_VMEM_LIMIT_BYTES)


def _rms(x, g):
    return x * lax.rsqrt(jnp.mean(x * x, axis=-1, keepdims=True) + EPS) * g


def _dot(a, b):
    return jnp.dot(a, b, preferred_element_type=F32)


def _dot_nt(a, b):
    return lax.dot_general(a, b, (((1,), (1,)), ((), ())), preferred_element_type=F32)


def _rep(x, n):
    return x if n == 1 else jnp.concatenate([x] * n, axis=1)


def _prenorm_kernel(x_ref, g_ref, o_ref):
    o_ref[...] = _rms(x_ref[...], g_ref[...]).astype(o_ref.dtype)


def _prenorm(x, g):
    t, d = x.shape
    tm = _tile(t, (416, 208, 128, 64, 32, 16))
    return pl.pallas_call(
        _prenorm_kernel,
        grid=(t // tm,),
        in_specs=[pl.BlockSpec((tm, d), lambda i: (i, 0)), pl.BlockSpec((1, d), lambda i: (0, 0))],
        out_specs=pl.BlockSpec((tm, d), lambda i: (i, 0)),
        out_shape=jax.ShapeDtypeStruct((t, d), BF16),
        compiler_params=_cp("parallel"),
        name="prenorm",
    )(x, g.reshape(1, d))


def _post_kernel(x_ref, y_ref, gp_ref, gn_ref, xo_ref, ho_ref, *, scale):
    xn = x_ref[...] + scale * _rms(y_ref[...], gp_ref[...])
    xo_ref[...] = xn
    ho_ref[...] = _rms(xn, gn_ref[...]).astype(ho_ref.dtype)


def _post_last_kernel(x_ref, y_ref, gp_ref, xp_ref, xs_ref, *, scale, n_prompt_tiles):
    i = pl.program_id(0)
    xn = x_ref[...] + scale * _rms(y_ref[...], gp_ref[...])

    @pl.when(i < n_prompt_tiles)
    def _():
        xp_ref[...] = xn

    @pl.when(i >= n_prompt_tiles)
    def _():
        xs_ref[...] = xn


def _post_last(x, y, g_post, scale, n_prompt):
    t, d = x.shape
    tm = t - n_prompt
    assert n_prompt % tm == 0 and tm % SUBLANES == 0
    npt = n_prompt // tm
    row = pl.BlockSpec((tm, d), lambda i: (i, 0))
    vec = pl.BlockSpec((1, d), lambda i: (0, 0))
    return pl.pallas_call(
        functools.partial(_post_last_kernel, scale=scale, n_prompt_tiles=npt),
        grid=(t // tm,),
        in_specs=[row, row, vec],
        out_specs=[pl.BlockSpec((tm, d), lambda i: (jnp.minimum(i, npt - 1), 0)),
                   pl.BlockSpec((tm, d), lambda i: (0, 0))],
        out_shape=[jax.ShapeDtypeStruct((n_prompt, d), F32), jax.ShapeDtypeStruct((tm, d), F32)],
        compiler_params=_cp("arbitrary"),
        name="post_last",
    )(x, y, g_post.reshape(1, d))


def _post(x, y, g_post, scale, g_next):
    t, d = x.shape
    tm = _tile(t, (208, 128, 64, 32, 16))
    row = pl.BlockSpec((tm, d), lambda i: (i, 0))
    vec = pl.BlockSpec((1, d), lambda i: (0, 0))
    return pl.pallas_call(
        functools.partial(_post_kernel, scale=scale),
        grid=(t // tm,),
        in_specs=[row, row, vec, vec],
        out_specs=[row, row],
        out_shape=[jax.ShapeDtypeStruct((t, d), F32), jax.ShapeDtypeStruct((t, d), BF16)],
        compiler_params=_cp("parallel"),
        name="post",
    )(x, y, g_post.reshape(1, d), g_next.reshape(1, d))


def _ffn_kernel(h_ref, wg_ref, wu_ref, wd_ref, o_ref, *, n_chunks):
    j = pl.program_id(1)

    @pl.when(j == 0)
    def _():
        o_ref[...] = jnp.zeros(o_ref.shape, o_ref.dtype)

    h = h_ref[...]
    g = _dot(h, wg_ref[...].astype(BF16))
    u = _dot(h, wu_ref[...].astype(BF16))
    a = (g * jax.nn.sigmoid(g) * u).astype(BF16)
    cw = o_ref.shape[1] // n_chunks
    for c in range(n_chunks):
        o_ref[:, c * cw:(c + 1) * cw] += _dot(a, wd_ref[:, c * cw:(c + 1) * cw].astype(BF16))


def _ffn(h, wg, wu, wd, layer, half):
    t, d = h.shape
    f = wg.shape[-1]
    tm = _tile(t, (1040, 512, 256, 128, 64, 32, 16))
    tf = _tile(f, (256, 128))
    n_chunks = max(1, d // 512)
    return pl.pallas_call(
        functools.partial(_ffn_kernel, n_chunks=n_chunks),
        grid=(t // tm, f // tf),
        in_specs=[
            pl.BlockSpec((tm, d), lambda i, j: (i, 0), pipeline_mode=pl.Buffered(1)),
            pl.BlockSpec((None, None, d, tf), lambda i, j: (layer, half, 0, j)),
            pl.BlockSpec((None, None, d, tf), lambda i, j: (layer, half, 0, j)),
            pl.BlockSpec((None, None, tf, d), lambda i, j: (layer, half, j, 0)),
        ],
        out_specs=pl.BlockSpec((tm, d), lambda i, j: (i, 0), pipeline_mode=pl.Buffered(1)),
        out_shape=jax.ShapeDtypeStruct((t, d), F32),
        compiler_params=pltpu.CompilerParams(dimension_semantics=("parallel", "arbitrary"),
                                             vmem_limit_bytes=FFN_VMEM_LIMIT_BYTES),
        name="ffn",
    )(h, wg, wu, wd)


def _mm_kernel(x_ref, w_ref, o_ref):
    o_ref[...] = _dot(x_ref[...], w_ref[...].astype(BF16)).astype(o_ref.dtype)


def _mm_specs(t, k, n):
    tm = _tile(t, (1040, 512, 256, 128, 64, 32, 16))
    tn = _tile(n, (512, 256, 128))
    return tm, tn


def _mm(x, w, n=None, col_off=0, out_dtype=F32):
    t, k = x.shape
    n = w.shape[1] if n is None else n
    tm, tn = _mm_specs(t, k, n)
    assert col_off % tn == 0
    cb = col_off // tn
    return pl.pallas_call(
        _mm_kernel,
        grid=(t // tm, n // tn),
        in_specs=[pl.BlockSpec((tm, k), lambda i, j: (i, 0)), pl.BlockSpec((k, tn), lambda i, j: (0, cb + j))],
        out_specs=pl.BlockSpec((tm, tn), lambda i, j: (i, j)),
        out_shape=jax.ShapeDtypeStruct((t, n), out_dtype),
        compiler_params=_cp("parallel", "arbitrary"),
        name="mm",
    )(x, w)


def _mm2_kernel(x1_ref, x2_ref, w_ref, o_ref):
    k1 = x1_ref.shape[1]
    o_ref[...] = (_dot(x1_ref[...], w_ref[:k1, :].astype(BF16))
                  + _dot(x2_ref[...], w_ref[k1:, :].astype(BF16)))


def _mm2(x1, x2, w):
    t, k1 = x1.shape
    k2 = x2.shape[1]
    n = w.shape[1]
    tm, tn = _mm_specs(t, k1 + k2, n)
    return pl.pallas_call(
        _mm2_kernel,
        grid=(t // tm, n // tn),
        in_specs=[pl.BlockSpec((tm, k1), lambda i, j: (i, 0)), pl.BlockSpec((tm, k2), lambda i, j: (i, 0)),
                  pl.BlockSpec((k1 + k2, tn), lambda i, j: (0, j))],
        out_specs=pl.BlockSpec((tm, tn), lambda i, j: (i, j)),
        out_shape=jax.ShapeDtypeStruct((t, n), F32),
        compiler_params=_cp("parallel", "arbitrary"),
        name="mm2",
    )(x1, x2, w)


def _mm_glu_kernel(zb_ref, w_ref, z_ref, o_ref):
    s = _dot(zb_ref[...], w_ref[...].astype(BF16))
    o_ref[...] = (z_ref[...] * jax.nn.sigmoid(s)).astype(o_ref.dtype)


def _mm_glu(zb, w, z):
    t, k = zb.shape
    n = w.shape[1]
    tm, tn = _mm_specs(t, k, n)
    return pl.pallas_call(
        _mm_glu_kernel,
        grid=(t // tm, n // tn),
        in_specs=[pl.BlockSpec((tm, k), lambda i, j: (i, 0)), pl.BlockSpec((k, tn), lambda i, j: (0, j)),
                  pl.BlockSpec((tm, tn), lambda i, j: (i, j))],
        out_specs=pl.BlockSpec((tm, tn), lambda i, j: (i, j)),
        out_shape=jax.ShapeDtypeStruct((t, n), BF16),
        compiler_params=_cp("parallel", "arbitrary"),
        name="mm_glu",
    )(zb, w, z)


def _pool_prompt_kernel(u_ref, wp_ref, ps_ref, o_ref):
    s_len = u_ref.shape[0]
    pg = wp_ref.shape[1]
    row = lax.broadcasted_iota(jnp.int32, (s_len, pg), 0)
    for gi, w in enumerate(POOL_WINDOWS):
        cols = slice(gi * pg, (gi + 1) * pg)
        u = u_ref[:, cols]
        s = u
        d = 1
        while d < w:
            s = s + jnp.where(row >= d, pltpu.roll(s, d, axis=0), 0.0)
            d *= 2
        cnt = jnp.minimum(row + 1, w).astype(F32)
        dd = (s / cnt - u).astype(BF16)
        o_ref[:, cols] = (_dot(dd, wp_ref[gi]) * ps_ref[:, cols]).astype(o_ref.dtype)


def _pool_prompt(u_pool, w_pool, pool_scale, batch, seq):
    c = u_pool.shape[1]
    return pl.pallas_call(
        _pool_prompt_kernel,
        grid=(batch,),
        in_specs=[pl.BlockSpec((seq, c), lambda b: (b, 0)),
                  pl.BlockSpec(w_pool.shape, lambda b: (0, 0, 0)),
                  pl.BlockSpec((1, c), lambda b: (0, 0))],
        out_specs=pl.BlockSpec((seq, c), lambda b: (b, 0)),
        out_shape=jax.ShapeDtypeStruct((batch * seq, c), BF16),
        compiler_params=_cp("parallel"),
        name="pool_prompt",
    )(u_pool, w_pool, pool_scale.reshape(1, c))


def _pool_sample_kernel(ext_ref, wp_ref, ps_ref, o_ref, *, start):
    nb = ext_ref.shape[0]
    pg = wp_ref.shape[1]
    for gi, w in enumerate(POOL_WINDOWS):
        cols = slice(gi * pg, (gi + 1) * pg)
        u = ext_ref[nb - 1, :, cols]
        s = u
        for i in range(1, w):
            s = s + ext_ref[nb - 1 - i, :, cols]
        dd = (s / float(min(start + 1, w)) - u).astype(BF16)
        o_ref[:, cols] = (_dot(dd, wp_ref[gi]) * ps_ref[:, cols]).astype(o_ref.dtype)


def _pool_sample(ext_t, w_pool, pool_scale, start):
    nb, bd, c = ext_t.shape
    return pl.pallas_call(
        functools.partial(_pool_sample_kernel, start=start),
        grid=(1,),
        in_specs=[pl.BlockSpec((nb, bd, c), lambda i: (0, 0, 0)),
                  pl.BlockSpec(w_pool.shape, lambda i: (0, 0, 0)),
                  pl.BlockSpec((1, c), lambda i: (0, 0))],
        out_specs=pl.BlockSpec((bd, c), lambda i: (0, 0)),
        out_shape=jax.ShapeDtypeStruct((bd, c), BF16),
        compiler_params=_cp("arbitrary"),
        name="pool_sample",
    )(ext_t, w_pool, pool_scale.reshape(1, c))


def _mla_prep_kernel(h_ref, wq_ref, wkv_ref, wkp_ref, wkps_ref, qn_ref, kvn_ref, cos_ref, sin_ref,
                     cq_ref, ckv_ref, ckvb_ref, kpe_ref, kpeb_ref):
    h = h_ref[...]
    cq_ref[...] = _rms(_dot(h, wq_ref[...].astype(BF16)), qn_ref[...]).astype(cq_ref.dtype)
    ckv = _rms(_dot(h, wkv_ref[...].astype(BF16)), kvn_ref[...])
    ckv_ref[...] = ckv
    ckvb_ref[...] = ckv.astype(ckvb_ref.dtype)
    kpe = _dot(h, wkp_ref[...]) * cos_ref[...] + _dot(h, wkps_ref[...]) * sin_ref[...]
    kpe_ref[...] = kpe
    kpeb_ref[...] = kpe.astype(kpeb_ref.dtype)


def _mla_prep(h, w_in0, o_q, o_kv, w_kpe, w_kpe_sw, q_norm, kv_norm, cosf, sinf):
    t, d = h.shape
    ql, kl, r = q_norm.shape[0], kv_norm.shape[0], w_kpe.shape[1]
    assert o_q % ql == 0 and o_kv % kl == 0
    tm = _tile(t, (416, 208, 128, 64, 32, 16))
    row = lambda n: pl.BlockSpec((tm, n), lambda i: (i, 0))
    full = lambda a, b: pl.BlockSpec((a, b), lambda i: (0, 0))
    win = lambda n, off: pl.BlockSpec((d, n), lambda i: (0, off // n), pipeline_mode=pl.Buffered(1))
    return pl.pallas_call(
        _mla_prep_kernel,
        grid=(t // tm,),
        in_specs=[row(d), win(ql, o_q), win(kl, o_kv), full(d, r), full(d, r), full(1, ql), full(1, kl),
                  row(r), row(r)],
        out_specs=[row(ql), row(kl), row(kl), row(r), row(r)],
        out_shape=[jax.ShapeDtypeStruct((t, ql), BF16), jax.ShapeDtypeStruct((t, kl), F32),
                   jax.ShapeDtypeStruct((t, kl), BF16), jax.ShapeDtypeStruct((t, r), F32),
                   jax.ShapeDtypeStruct((t, r), BF16)],
        compiler_params=_cp("parallel"),
        name="mla_prep",
    )(h, w_in0, w_in0, w_kpe, w_kpe_sw, q_norm.reshape(1, ql), kv_norm.reshape(1, kl), cosf, sinf)


def _q_prep_kernel(cq_ref, wqn_ref, wqr_ref, wqrs_ref, wuk_ref, cos_ref, sin_ref, ql_ref, qp_ref):
    cq = cq_ref[...]
    q_nope = _dot(cq, wqn_ref[0]).astype(BF16)
    ql_ref[0] = _dot(q_nope, wuk_ref[0]).astype(ql_ref.dtype)
    q_pe = _dot(cq, wqr_ref[0]) * cos_ref[...] + _dot(cq, wqrs_ref[0]) * sin_ref[...]
    qp_ref[0] = q_pe.astype(qp_ref.dtype)


def _q_prep(cq, w_qn, w_qr, w_qr_sw, w_uk_t, cosf, sinf):
    t, ql = cq.shape
    heads, _, nope = w_qn.shape
    r = w_qr.shape[2]
    kl = w_uk_t.shape[2]
    tm = _tile(t, (1040, 512, 256, 128, 64, 32, 16))
    hw = lambda a, b: pl.BlockSpec((1, a, b), lambda i, h: (h, 0, 0))
    return pl.pallas_call(
        _q_prep_kernel,
        grid=(t // tm, heads),
        in_specs=[pl.BlockSpec((tm, ql), lambda i, h: (i, 0)), hw(ql, nope), hw(ql, r), hw(ql, r), hw(nope, kl),
                  pl.BlockSpec((tm, r), lambda i, h: (i, 0)), pl.BlockSpec((tm, r), lambda i, h: (i, 0))],
        out_specs=[pl.BlockSpec((1, tm, kl), lambda i, h: (h, i, 0)),
                   pl.BlockSpec((1, tm, r), lambda i, h: (h, i, 0))],
        out_shape=[jax.ShapeDtypeStruct((heads, t, kl), BF16), jax.ShapeDtypeStruct((heads, t, r), BF16)],
        compiler_params=_cp("parallel", "arbitrary"),
        name="q_prep",
    )(cq, w_qn, w_qr, w_qr_sw, w_uk_t, cosf, sinf)


def _o_proj_kernel(ol_ref, wuv_ref, o_ref):
    o_ref[...] = _dot(ol_ref[0], wuv_ref[0]).astype(o_ref.dtype)


def _o_proj(o_lat, w_uv_t):
    heads, t, kl = o_lat.shape
    vh = w_uv_t.shape[2]
    tm = _tile(t, (1024, 512, 256, 128, 64, 32, 16))
    return pl.pallas_call(
        _o_proj_kernel,
        grid=(t // tm, heads),
        in_specs=[pl.BlockSpec((1, tm, kl), lambda i, h: (h, i, 0)), pl.BlockSpec((1, kl, vh), lambda i, h: (h, 0, 0))],
        out_specs=pl.BlockSpec((tm, vh), lambda i, h: (i, h)),
        out_shape=jax.ShapeDtypeStruct((t, heads * vh), BF16),
        compiler_params=_cp("parallel", "arbitrary"),
        name="o_proj",
    )(o_lat, w_uv_t)


def _softmax_update(s, v, m_ref, l_ref, acc_ref):
    m_prev = m_ref[...]
    m_new = jnp.maximum(m_prev, jnp.max(s, axis=1, keepdims=True))
    alpha = jnp.exp(m_prev - m_new)
    p = jnp.exp(s - _rep(m_new, s.shape[1] // LANES))
    l_ref[...] = alpha * l_ref[...] + jnp.sum(p, axis=1, keepdims=True)
    acc_ref[...] = acc_ref[...] * _rep(alpha, acc_ref.shape[1] // LANES) + _dot(p.astype(BF16), v)
    m_ref[...] = m_new


def _attn_prompt_kernel(ql_ref, qp_ref, k_ref, kp_ref, o_ref, m_ref, l_ref, acc_ref, *, scale):
    qi = pl.program_id(1)
    hg, tq, c = ql_ref.shape
    r = hg * tq
    q = ql_ref[...].reshape(r, c)
    qp = qp_ref[...].reshape(r, qp_ref.shape[2])
    m_ref[...] = jnp.full(m_ref.shape, NEG_INF, F32)
    l_ref[...] = jnp.zeros(l_ref.shape, F32)
    acc_ref[...] = jnp.zeros(acc_ref.shape, F32)

    def block(start, width, masked):
        k = k_ref[pl.ds(start, width), :]
        kp = kp_ref[pl.ds(start, width), :]
        s = (_dot_nt(q, k) + _dot_nt(qp, kp)) * scale
        if masked:
            s3 = s.reshape(hg, tq, width)
            qpos = qi * tq + lax.broadcasted_iota(jnp.int32, s3.shape, 1)
            kpos = start + lax.broadcasted_iota(jnp.int32, s3.shape, 2)
            s = jnp.where(kpos <= qpos, s3, NEG_INF).reshape(r, width)
        _softmax_update(s, k, m_ref, l_ref, acc_ref)

    def body(kb, carry):
        block(pl.multiple_of(kb * (2 * tq), 2 * tq), 2 * tq, False)
        return carry

    lax.fori_loop(0, qi // 2, body, 0)

    @pl.when(qi % 2 == 0)
    def _():
        block(pl.multiple_of(qi * tq, tq), tq, True)

    @pl.when(qi % 2 == 1)
    def _():
        block(pl.multiple_of((qi - 1) * tq, 2 * tq), 2 * tq, True)

    o = acc_ref[...] / _rep(l_ref[...], c // LANES)
    o_ref[...] = o.astype(o_ref.dtype).reshape(hg, tq, c)


def _attn_prompt(q_lat, q_pe, ckv_b, kpe_b, batch, seq, scale):
    heads, _, c = q_lat.shape
    r = q_pe.shape[2]
    tq = _tile(seq, (256, 128))
    hg = _tile(heads, (12, 8, 4, 2, 1))
    nq = seq // tq
    rows = hg * tq
    return pl.pallas_call(
        functools.partial(_attn_prompt_kernel, scale=scale),
        grid=(batch, nq, heads // hg),
        in_specs=[pl.BlockSpec((hg, tq, c), lambda b, i, g: (g, b * nq + i, 0)),
                  pl.BlockSpec((hg, tq, r), lambda b, i, g: (g, b * nq + i, 0)),
                  pl.BlockSpec((seq, c), lambda b, i, g: (b, 0)),
                  pl.BlockSpec((seq, r), lambda b, i, g: (b, 0))],
        out_specs=pl.BlockSpec((hg, tq, c), lambda b, i, g: (g, b * nq + i, 0)),
        out_shape=jax.ShapeDtypeStruct((heads, batch * seq, c), BF16),
        scratch_shapes=[pltpu.VMEM((rows, LANES), F32), pltpu.VMEM((rows, LANES), F32),
                        pltpu.VMEM((rows, c), F32)],
        compiler_params=_cp("parallel", "parallel", "arbitrary"),
        name="attn_prompt",
    )(q_lat, q_pe, ckv_b, kpe_b)


def _attn_sample_kernel(pt_ref, ql_ref, qp_ref, cn_ref, pn_ref, *rest, pps, scale):
    del pt_ref
    ck_refs, kp_refs = rest[:pps], rest[pps:2 * pps]
    o_ref, m_ref, l_ref, acc_ref = rest[2 * pps:]
    step = pl.program_id(1)
    q = ql_ref[0]
    qp = qp_ref[0]

    @pl.when(step == 0)
    def _():
        cn = cn_ref[0]
        pn = pn_ref[0]
        s0 = (jnp.sum(q.astype(F32) * cn, axis=1, keepdims=True)
              + jnp.sum(qp.astype(F32) * pn, axis=1, keepdims=True)) * scale
        m_ref[...] = jnp.broadcast_to(s0, m_ref.shape)
        l_ref[...] = jnp.ones(l_ref.shape, F32)
        acc_ref[...] = jnp.broadcast_to(cn, acc_ref.shape)

    k = jnp.concatenate([ref[0].astype(BF16) for ref in ck_refs], axis=0)
    kp_t = jnp.concatenate([ref[0].astype(BF16) for ref in kp_refs], axis=1)
    s = (_dot_nt(q, k) + _dot(qp, kp_t)) * scale
    _softmax_update(s, k, m_ref, l_ref, acc_ref)

    @pl.when(step == pl.num_programs(1) - 1)
    def _():
        o_ref[0] = (acc_ref[...] / _rep(l_ref[...], acc_ref.shape[1] // LANES)).astype(o_ref.dtype)


def _attn_sample(q_lat, q_pe, ckv_new, kpe_new, cache_ckv, cache_kpe_t, page_table, scale):
    bd, heads, c = q_lat.shape
    r = q_pe.shape[2]
    n_pages = page_table.shape[1]
    page = cache_ckv.shape[1]
    pps = _tile(n_pages, (32, 16, 8, 4, 2, 1))

    def page_map(i, b, s, pt):
        return (pt[b * n_pages + s * pps + i], 0, 0)

    per_seq = lambda n, w: pl.BlockSpec((1, n, w), lambda b, s, pt: (b, 0, 0))
    in_specs = [per_seq(heads, c), per_seq(heads, r), per_seq(1, c), per_seq(1, r)]
    in_specs += [pl.BlockSpec((1, page, c), functools.partial(page_map, i)) for i in range(pps)]
    in_specs += [pl.BlockSpec((1, r, page), functools.partial(page_map, i)) for i in range(pps)]
    return pl.pallas_call(
        functools.partial(_attn_sample_kernel, pps=pps, scale=scale),
        grid_spec=pltpu.PrefetchScalarGridSpec(
            num_scalar_prefetch=1,
            grid=(bd, n_pages // pps),
            in_specs=in_specs,
            out_specs=per_seq(heads, c),
            scratch_shapes=[pltpu.VMEM((heads, LANES), F32), pltpu.VMEM((heads, LANES), F32),
                            pltpu.VMEM((heads, c), F32)],
        ),
        out_shape=jax.ShapeDtypeStruct((bd, heads, c), BF16),
        compiler_params=_cp("parallel", "arbitrary"),
        name="attn_sample",
    )(page_table.reshape(-1), q_lat, q_pe, ckv_new, kpe_new, *([cache_ckv] * pps), *([cache_kpe_t] * pps))


def _s5_disc_kernel(lr_ref, li_ref, ldt_ref, br_ref, bi_ref, bbr_ref, bbi_ref, pr_ref, pi_ref):
    dt = jnp.exp(ldt_ref[...])
    lr = lr_ref[...]
    li = li_ref[...]
    mag = jnp.exp(lr * dt)
    a_re = mag * jnp.cos(li * dt)
    a_im = mag * jnp.sin(li * dt)
    den = lr * lr + li * li
    nr, ni = a_re - 1.0, a_im
    f_re = (nr * lr + ni * li) / den
    f_im = (ni * lr - nr * li) / den
    for k in range(br_ref.shape[0]):
        bbr_ref[k] = f_re * br_ref[k] - f_im * bi_ref[k]
        bbi_ref[k] = f_re * bi_ref[k] + f_im * br_ref[k]
    p_re, p_im = a_re, a_im
    for s in range(pr_ref.shape[0]):
        pr_ref[s] = p_re
        pi_ref[s] = p_im
        p_re, p_im = p_re * a_re - p_im * a_im, p_re * a_im + p_im * a_re


def _s5_disc(lam_re, lam_im, log_dt, b_re_t, b_im_t):
    k, g, n = b_re_t.shape
    spec2 = pl.BlockSpec((g, n), lambda i: (0, 0))
    spec3 = lambda a: pl.BlockSpec((a, g, n), lambda i: (0, 0, 0))
    return pl.pallas_call(
        _s5_disc_kernel,
        grid=(1,),
        in_specs=[spec2, spec2, spec2, spec3(k), spec3(k)],
        out_specs=[spec3(k), spec3(k), spec3(SUBLANES), spec3(SUBLANES)],
        out_shape=[jax.ShapeDtypeStruct((k, g, n), F32)] * 2 + [jax.ShapeDtypeStruct((SUBLANES, g, n), F32)] * 2,
        compiler_params=_cp("arbitrary"),
        name="s5_disc",
    )(lam_re, lam_im, jnp.broadcast_to(log_dt[:, None], (g, n)), b_re_t, b_im_t)


def _s5_tail(h_re, h_im, u, cr_ref, ci_ref, d_ref, z_ref, zb_ref):
    y = _dot(h_re.astype(BF16), cr_ref[0]) - _dot(h_im.astype(BF16), ci_ref[0]) + d_ref[...] * u
    z = jax.nn.gelu(y, approximate=True)
    z_ref[...] = z
    zb_ref[...] = z.astype(zb_ref.dtype)


def _s5_scan_kernel(u_ref, bbr_ref, bbi_ref, pr_ref, pi_ref, cr_ref, ci_ref, d_ref, h0r_ref, h0i_ref,
                    z_ref, zb_ref, hr_out_ref, hi_out_ref, sr_ref, si_ref, cr_carry, ci_carry):
    tc = pl.program_id(2)
    tt, w = sr_ref.shape

    @pl.when(tc == 0)
    def _():
        cr_carry[...] = h0r_ref[0, 0]
        ci_carry[...] = h0i_ref[0, 0]

    u = u_ref[...]
    ub = u.astype(BF16)
    sr_ref[...] = _dot(ub, bbr_ref[0])
    si_ref[...] = _dot(ub, bbi_ref[0])
    p_re = pr_ref[0]
    p_im = pi_ref[0]
    sub = lax.broadcasted_iota(jnp.int32, (SUBLANES, w), 0)
    steps = []
    d = 1
    while d < SUBLANES:
        steps.append((d, jnp.where(sub >= d, p_re[d - 1:d, :], 0.0), jnp.where(sub >= d, p_im[d - 1:d, :], 0.0)))
        d *= 2

    def tile(i, carry):
        c_re, c_im = carry
        off = pl.multiple_of(i * SUBLANES, SUBLANES)
        x_re = sr_ref[pl.ds(off, SUBLANES), :]
        x_im = si_ref[pl.ds(off, SUBLANES), :]
        for d, ad_re, ad_im in steps:
            s_re = pltpu.roll(x_re, d, axis=0)
            s_im = pltpu.roll(x_im, d, axis=0)
            x_re, x_im = x_re + ad_re * s_re - ad_im * s_im, x_im + ad_re * s_im + ad_im * s_re
        x_re, x_im = x_re + p_re * c_re - p_im * c_im, x_im + p_re * c_im + p_im * c_re
        sr_ref[pl.ds(off, SUBLANES), :] = x_re
        si_ref[pl.ds(off, SUBLANES), :] = x_im
        return x_re[SUBLANES - 1:, :], x_im[SUBLANES - 1:, :]

    c_re, c_im = lax.fori_loop(0, tt // SUBLANES, tile, (cr_carry[...], ci_carry[...]))
    cr_carry[...] = c_re
    ci_carry[...] = c_im
    _s5_tail(sr_ref[...], si_ref[...], u, cr_ref, ci_ref, d_ref, z_ref, zb_ref)

    @pl.when(tc == pl.num_programs(2) - 1)
    def _():
        hr_out_ref[0, 0] = c_re
        hi_out_ref[0, 0] = c_im


def _s5_scan(u, row_off, batch, seq, bb_re, bb_im, pw_re, pw_im, cm_re, cm_im, d_skip, h0_re, h0_im):
    n_chunks, cw_in, cw_st = bb_re.shape
    tt = _tile(seq, (512, 256, 128, 64, 32, 16, 8))
    nt = seq // tt
    ob = row_off // tt
    cmat = lambda a, b: pl.BlockSpec((1, a, b), lambda b_, c, t: (c, 0, 0))
    st = pl.BlockSpec((1, 1, 1, cw_st), lambda b_, c, t: (b_, c, 0, 0))
    urow = pl.BlockSpec((tt, cw_in), lambda b_, c, t: (ob + b_ * nt + t, c))
    orow = urow
    rows = u.shape[0]
    wd = n_chunks * cw_in
    return pl.pallas_call(
        _s5_scan_kernel,
        grid=(batch, n_chunks, nt),
        in_specs=[urow, cmat(cw_in, cw_st), cmat(cw_in, cw_st), cmat(SUBLANES, cw_st), cmat(SUBLANES, cw_st),
                  cmat(cw_st, cw_in), cmat(cw_st, cw_in), pl.BlockSpec((1, cw_in), lambda b_, c, t: (0, c)), st, st],
        out_specs=[orow, orow, st, st],
        out_shape=[jax.ShapeDtypeStruct((rows, wd), F32), jax.ShapeDtypeStruct((rows, wd), BF16),
                   jax.ShapeDtypeStruct(h0_re.shape, F32), jax.ShapeDtypeStruct(h0_im.shape, F32)],
        scratch_shapes=[pltpu.VMEM((tt, cw_st), F32), pltpu.VMEM((tt, cw_st), F32),
                        pltpu.VMEM((1, cw_st), F32), pltpu.VMEM((1, cw_st), F32)],
        compiler_params=_cp("parallel", "parallel", "arbitrary"),
        name="s5_scan",
    )(u, bb_re, bb_im, pw_re, pw_im, cm_re, cm_im, d_skip.reshape(1, wd), h0_re, h0_im)


def _s5_step_kernel(u_ref, bbr_ref, bbi_ref, pr_ref, pi_ref, cr_ref, ci_ref, d_ref, h0r_ref, h0i_ref,
                    z_in_ref, zb_in_ref, z_ref, zb_ref, hr_out_ref, hi_out_ref):
    del z_in_ref, zb_in_ref
    u = u_ref[...]
    ub = u.astype(BF16)
    a_re = pr_ref[0][0:1, :]
    a_im = pi_ref[0][0:1, :]
    h0r = h0r_ref[...]
    h0i = h0i_ref[...]
    h_re = _dot(ub, bbr_ref[0]) + (a_re * h0r - a_im * h0i)
    h_im = _dot(ub, bbi_ref[0]) + (a_re * h0i + a_im * h0r)
    hr_out_ref[...] = h_re
    hi_out_ref[...] = h_im
    _s5_tail(h_re, h_im, u, cr_ref, ci_ref, d_ref, z_ref, zb_ref)


def _s5_step(u, row_off, n_seq, bb_re, bb_im, pw_re, pw_im, cm_re, cm_im, d_skip, h0_re, h0_im, z, zb):
    n_chunks, cw_in, cw_st = bb_re.shape
    ob = row_off // n_seq
    cmat = lambda a, b: pl.BlockSpec((1, a, b), lambda c: (c, 0, 0))
    st = pl.BlockSpec((n_seq, cw_st), lambda c: (0, c))
    urow = pl.BlockSpec((n_seq, cw_in), lambda c: (ob, c))
    wd = n_chunks * cw_in
    anyspec = pl.BlockSpec(memory_space=pl.ANY)
    return pl.pallas_call(
        _s5_step_kernel,
        grid=(n_chunks,),
        in_specs=[urow, cmat(cw_in, cw_st), cmat(cw_in, cw_st),
                  cmat(SUBLANES, cw_st), cmat(SUBLANES, cw_st), cmat(cw_st, cw_in), cmat(cw_st, cw_in),
                  pl.BlockSpec((1, cw_in), lambda c: (0, c)), st, st, anyspec, anyspec],
        out_specs=[urow, urow, st, st],
        out_shape=[jax.ShapeDtypeStruct(z.shape, z.dtype), jax.ShapeDtypeStruct(zb.shape, zb.dtype),
                   jax.ShapeDtypeStruct(h0_re.shape, F32), jax.ShapeDtypeStruct(h0_im.shape, F32)],
        input_output_aliases={10: 0, 11: 1},
        compiler_params=_cp("parallel"),
        name="s5_step",
    )(u, bb_re, bb_im, pw_re, pw_im, cm_re, cm_im, d_skip.reshape(1, wd), h0_re, h0_im, z, zb)


def _block_diag(m, gpc):
    g, a, b = m.shape
    eye = jnp.eye(gpc, dtype=m.dtype)
    mc = m.reshape(g // gpc, gpc, a, b)
    blk = mc[:, :, :, None, :] * eye[None, :, None, :, None]
    return blk.reshape(g // gpc, gpc * a, gpc * b)


def kernel(x_prompt, x_sample, cache_ckv, cache_kpe, page_table, state_pool, state_ssm_re, state_ssm_im, norm_gains, w_ffn_gate, w_ffn_up, w_ffn_down, w_in0, w_pool, pool_scale, q_norm, kv_norm, w_uq, w_uk, w_uv, w_out0, w_in1, lam_re, lam_im, log_dt, b_re, b_im, c_re, c_im, d_skip, w_glu, w_out1):
    batch, seq, d_model = x_prompt.shape
    bd, dec_seq, _ = x_sample.shape
    assert dec_seq == 1, "the sample group carries one new token per sequence"
    n_prompt = batch * seq
    n_tok = n_prompt + bd
    pool_buf, pool_width = state_pool.shape[1], state_pool.shape[2]
    assert seq >= pool_buf and n_prompt % bd == 0
    kv_lora, heads, qk_nope = w_uk.shape
    v_head = w_uv.shape[2]
    q_lora = q_norm.shape[0]
    qk_rope = cache_kpe.shape[2]
    half = qk_rope // 2
    past_len = page_table.shape[1] * cache_ckv.shape[1]
    sm_scale = float(qk_nope + qk_rope) ** -0.5
    groups, n_state, grp = b_re.shape
    depth = norm_gains.shape[0]
    assert depth == 2

    wg, wu, wd = w_ffn_gate, w_ffn_up, w_ffn_down
    o_q, o_kv, o_pe = pool_width, pool_width + q_lora, pool_width + q_lora + kv_lora
    swap = jnp.concatenate([jnp.arange(half, qk_rope), jnp.arange(half)])
    w_kpe = w_in0[:, o_pe:].astype(BF16)
    w_kpe_sw = w_kpe[:, swap]
    w_uq3 = w_uq.astype(BF16).reshape(q_lora, heads, qk_nope + qk_rope).transpose(1, 0, 2)
    w_qn, w_qr = w_uq3[:, :, :qk_nope], w_uq3[:, :, qk_nope:]
    w_qr_sw = w_qr[:, :, swap]
    w_uk_t = w_uk.astype(BF16).transpose(1, 2, 0)
    w_uv_t = w_uv.astype(BF16).transpose(1, 0, 2)
    w_poolb = w_pool.astype(BF16)
    cache_kpe_t = cache_kpe.transpose(0, 2, 1)

    inv_freq = ROPE_THETA ** (-jnp.arange(half, dtype=F32) / half)
    pos = jnp.concatenate([jnp.tile(jnp.arange(seq), batch), jnp.full((bd,), past_len)]).astype(F32)
    ang = pos[:, None] * inv_freq[None, :]
    cos, sin = jnp.cos(ang), jnp.sin(ang)
    cosf = jnp.concatenate([cos, cos], axis=1)
    sinf = jnp.concatenate([-sin, sin], axis=1)

    x = jnp.concatenate([x_prompt.reshape(n_prompt, d_model), x_sample.reshape(bd, d_model)], axis=0)

    g = norm_gains[0]
    h = _prenorm(x, g[0])
    x, h = _post(x, _ffn(h, wg, wu, wd, 0, 0), g[1], 0.5, g[2])

    u_pool = _mm(h, w_in0, n=pool_width)
    cq, ckv, ckv_b, kpe, kpe_b = _mla_prep(h, w_in0, o_q, o_kv, w_kpe, w_kpe_sw, q_norm, kv_norm, cosf, sinf)
    y_pool_p = _pool_prompt(u_pool, w_poolb, pool_scale, batch, seq)
    ext_t = jnp.concatenate([state_pool.transpose(1, 0, 2), u_pool[None, n_prompt:]], axis=0)
    y_pool_s = _pool_sample(ext_t, w_poolb, pool_scale, past_len)
    y_pool = jnp.concatenate([y_pool_p, y_pool_s], axis=0)

    q_lat, q_pe = _q_prep(cq, w_qn, w_qr, w_qr_sw, w_uk_t, cosf, sinf)
    o_lat_p = _attn_prompt(q_lat, q_pe, ckv_b, kpe_b, batch, seq, sm_scale)
    o_lat_s = _attn_sample(q_lat[:, n_prompt:].transpose(1, 0, 2), q_pe[:, n_prompt:].transpose(1, 0, 2),
                           ckv[n_prompt:, None, :], kpe[n_prompt:, None, :],
                           cache_ckv, cache_kpe_t, page_table, sm_scale)
    o = jnp.concatenate([_o_proj(o_lat_p, w_uv_t), _o_proj(o_lat_s.transpose(1, 0, 2), w_uv_t)], axis=0)
    y = _mm2(y_pool, o, w_out0)

    x, h = _post(x, y, g[3], 1.0, g[4])
    g1 = norm_gains[1]
    x, h = _post(x, _ffn(h, wg, wu, wd, 0, 1), g[5], 0.5, g1[0])

    x, h = _post(x, _ffn(h, wg, wu, wd, 1, 0), g1[1], 0.5, g1[2])

    u = _mm(h, w_in1)
    gpc = _tile(groups, (S5_CHUNK_GROUPS, 8, 4, 2, 1))
    n_chunks = groups // gpc
    cw_st = gpc * n_state
    bbt_re, bbt_im, pw_re, pw_im = _s5_disc(lam_re, lam_im, log_dt, b_re.transpose(2, 0, 1), b_im.transpose(2, 0, 1))
    bb_re = _block_diag(bbt_re.transpose(1, 0, 2), gpc).astype(BF16)
    bb_im = _block_diag(bbt_im.transpose(1, 0, 2), gpc).astype(BF16)
    cm_re = _block_diag(c_re.transpose(0, 2, 1), gpc).astype(BF16)
    cm_im = _block_diag(c_im.transpose(0, 2, 1), gpc).astype(BF16)
    pw_re = pw_re.reshape(SUBLANES, n_chunks, cw_st).transpose(1, 0, 2)
    pw_im = pw_im.reshape(SUBLANES, n_chunks, cw_st).transpose(1, 0, 2)
    zero_h = jnp.zeros((batch, n_chunks, 1, cw_st), F32)
    z, zb, hr_p, hi_p = _s5_scan(u, 0, batch, seq, bb_re, bb_im, pw_re, pw_im, cm_re, cm_im, d_skip,
                                 zero_h, zero_h)
    z, zb, hr_s, hi_s = _s5_step(u, n_prompt, bd, bb_re, bb_im, pw_re, pw_im, cm_re, cm_im, d_skip,
                                 state_ssm_re.reshape(bd, groups * n_state),
                                 state_ssm_im.reshape(bd, groups * n_state), z, zb)
    y = _mm(_mm_glu(zb, w_glu, z), w_out1)

    x, h = _post(x, y, g1[3], 1.0, g1[4])
    xp, xs = _post_last(x, _ffn(h, wg, wu, wd, 1, 1), g1[5], 0.5, n_prompt)

    y_prompt = xp.reshape(batch, seq, d_model)
    y_sample = xs.reshape(bd, 1, d_model)
    pool_p = u_pool[:n_prompt].reshape(batch, seq, pool_width)[:, seq - pool_buf:]
    pool_s = jnp.concatenate([state_pool[:, 1:], u_pool[n_prompt:, None, :]], axis=1)
    ckv_p = ckv[:n_prompt].reshape(batch, seq, kv_lora)
    ckv_s = ckv[n_prompt:].reshape(bd, 1, kv_lora)
    kpe_p = kpe[:n_prompt].reshape(batch, seq, qk_rope)
    kpe_s = kpe[n_prompt:].reshape(bd, 1, qk_rope)
    return (y_prompt, y_sample, pool_p, pool_s, ckv_p, ckv_s, kpe_p, kpe_s,
            hr_p.reshape(batch, groups, n_state), hi_p.reshape(batch, groups, n_state),
            hr_s.reshape(bd, groups, n_state), hi_s.reshape(bd, groups, n_state))
```

```python
import functools

import jax
import jax.numpy as jnp
from jax import lax
from jax.experimental import pallas as pl
from jax.experimental.pallas import tpu as pltpu

F32 = jnp.float32
BF16 = jnp.bfloat16

EPS = 1e-6
ROPE_THETA = 10000.0
POOL_WINDOWS = (2, 4, 8, 16)
NEG_INF = -1e30

V7X_VMEM_LIMIT_BYTES = 56 * 1024 * 1024
FFN_VMEM_LIMIT_BYTES = 58 * 1024 * 1024
LANES = 128
SUBLANES = 8
S5_CHUNK_GROUPS = 16


def _tile(n, prefs):
    for p in prefs:
        if n % p == 0:
            return p
    return n


def _cp(*sem):
    return pltpu.CompilerParams(dimension_semantics=sem, vmem_limit_bytes=V7X_VMEM_LIMIT_BYTES)


def _rms(x, g):
    return x * lax.rsqrt(jnp.mean(x * x, axis=-1, keepdims=True) + EPS) * g


def _dot(a, b):
    return jnp.dot(a, b, preferred_element_type=F32)


def _dot_nt(a, b):
    return lax.dot_general(a, b, (((1,), (1,)), ((), ())), preferred_element_type=F32)


def _rep(x, n):
    return x if n == 1 else jnp.concatenate([x] * n, axis=1)


def _prenorm_kernel(x_ref, g_ref, o_ref):
    o_ref[...] = _rms(x_ref[...], g_ref[...]).astype(o_ref.dtype)


def _prenorm(x, g):
    t, d = x.shape
    tm = _tile(t, (416, 208, 128, 64, 32, 16))
    return pl.pallas_call(
        _prenorm_kernel,
        grid=(t // tm,),
        in_specs=[pl.BlockSpec((tm, d), lambda i: (i, 0)), pl.BlockSpec((1, d), lambda i: (0, 0))],
        out_specs=pl.BlockSpec((tm, d), lambda i: (i, 0)),
        out_shape=jax.ShapeDtypeStruct((t, d), BF16),
        compiler_params=_cp("parallel"),
        name="prenorm",
    )(x, g.reshape(1, d))


def _post_kernel(x_ref, y_ref, gp_ref, gn_ref, xo_ref, ho_ref, *, scale):
    xn = x_ref[...] + scale * _rms(y_ref[...], gp_ref[...])
    xo_ref[...] = xn
    ho_ref[...] = _rms(xn, gn_ref[...]).astype(ho_ref.dtype)


def _post_last_kernel(x_ref, y_ref, gp_ref, xp_ref, xs_ref, *, scale, n_prompt_tiles):
    i = pl.program_id(0)
    xn = x_ref[...] + scale * _rms(y_ref[...], gp_ref[...])

    @pl.when(i < n_prompt_tiles)
    def _():
        xp_ref[...] = xn

    @pl.when(i >= n_prompt_tiles)
    def _():
        xs_ref[...] = xn


def _post_last(x, y, g_post, scale, n_prompt):
    t, d = x.shape
    tm = t - n_prompt
    assert n_prompt % tm == 0 and tm % SUBLANES == 0
    npt = n_prompt // tm
    row = pl.BlockSpec((tm, d), lambda i: (i, 0))
    vec = pl.BlockSpec((1, d), lambda i: (0, 0))
    return pl.pallas_call(
        functools.partial(_post_last_kernel, scale=scale, n_prompt_tiles=npt),
        grid=(t // tm,),
        in_specs=[row, row, vec],
        out_specs=[pl.BlockSpec((tm, d), lambda i: (jnp.minimum(i, npt - 1), 0)),
                   pl.BlockSpec((tm, d), lambda i: (0, 0))],
        out_shape=[jax.ShapeDtypeStruct((n_prompt, d), F32), jax.ShapeDtypeStruct((tm, d), F32)],
        compiler_params=_cp("arbitrary"),
        name="post_last",
    )(x, y, g_post.reshape(1, d))


def _post(x, y, g_post, scale, g_next):
    t, d = x.shape
    tm = _tile(t, (208, 128, 64, 32, 16))
    row = pl.BlockSpec((tm, d), lambda i: (i, 0))
    vec = pl.BlockSpec((1, d), lambda i: (0, 0))
    return pl.pallas_call(
        functools.partial(_post_kernel, scale=scale),
        grid=(t // tm,),
        in_specs=[row, row, vec, vec],
        out_specs=[row, row],
        out_shape=[jax.ShapeDtypeStruct((t, d), F32), jax.ShapeDtypeStruct((t, d), BF16)],
        compiler_params=_cp("parallel"),
        name="post",
    )(x, y, g_post.reshape(1, d), g_next.reshape(1, d))


def _ffn_kernel(h_ref, wg_ref, wu_ref, wd_ref, o_ref, *, n_chunks):
    j = pl.program_id(1)

    @pl.when(j == 0)
    def _():
        o_ref[...] = jnp.zeros(o_ref.shape, o_ref.dtype)

    h = h_ref[...]
    g = _dot(h, wg_ref[...].astype(BF16))
    u = _dot(h, wu_ref[...].astype(BF16))
    a = (g * jax.nn.sigmoid(g) * u).astype(BF16)
    cw = o_ref.shape[1] // n_chunks
    for c in range(n_chunks):
        o_ref[:, c * cw:(c + 1) * cw] += _dot(a, wd_ref[:, c * cw:(c + 1) * cw].astype(BF16))


def _ffn(h, wg, wu, wd, layer, half):
    t, d = h.shape
    f = wg.shape[-1]
    tm = _tile(t, (1040, 512, 256, 128, 64, 32, 16))
    tf = _tile(f, (256, 128))
    n_chunks = max(1, d // 512)
    return pl.pallas_call(
        functools.partial(_ffn_kernel, n_chunks=n_chunks),
        grid=(t // tm, f // tf),
        in_specs=[
            pl.BlockSpec((tm, d), lambda i, j: (i, 0), pipeline_mode=pl.Buffered(1)),
            pl.BlockSpec((None, None, d, tf), lambda i, j: (layer, half, 0, j)),
            pl.BlockSpec((None, None, d, tf), lambda i, j: (layer, half, 0, j)),
            pl.BlockSpec((None, None, tf, d), lambda i, j: (layer, half, j, 0)),
        ],
        out_specs=pl.BlockSpec((tm, d), lambda i, j: (i, 0), pipeline_mode=pl.Buffered(1)),
        out_shape=jax.ShapeDtypeStruct((t, d), F32),
        compiler_params=pltpu.CompilerParams(dimension_semantics=("parallel", "arbitrary"),
                                             vmem_limit_bytes=FFN_VMEM_LIMIT_BYTES),
        name="ffn",
    )(h, wg, wu, wd)


def _mm_kernel(x_ref, w_ref, o_ref):
    o_ref[...] = _dot(x_ref[...], w_ref[...].astype(BF16)).astype(o_ref.dtype)


def _mm_specs(t, k, n):
    tm = _tile(t, (1040, 512, 256, 128, 64, 32, 16))
    tn = _tile(n, (512, 256, 128))
    return tm, tn


def _mm(x, w):
    t, k = x.shape
    n = w.shape[1]
    tm, tn = _mm_specs(t, k, n)
    return pl.pallas_call(
        _mm_kernel,
        grid=(t // tm, n // tn),
        in_specs=[pl.BlockSpec((tm, k), lambda i, j: (i, 0)), pl.BlockSpec((k, tn), lambda i, j: (0, j))],
        out_specs=pl.BlockSpec((tm, tn), lambda i, j: (i, j)),
        out_shape=jax.ShapeDtypeStruct((t, n), F32),
        compiler_params=_cp("parallel", "arbitrary"),
        name="mm",
    )(x, w)


def _mm2_kernel(x1_ref, x2_ref, w_ref, o_ref):
    k1 = x1_ref.shape[1]
    o_ref[...] = (_dot(x1_ref[...], w_ref[:k1, :].astype(BF16))
                  + _dot(x2_ref[...], w_ref[k1:, :].astype(BF16)))


def _mm2(x1, x2, w):
    t, k1 = x1.shape
    k2 = x2.shape[1]
    n = w.shape[1]
    tm, tn = _mm_specs(t, k1 + k2, n)
    return pl.pallas_call(
        _mm2_kernel,
        grid=(t // tm, n // tn),
        in_specs=[pl.BlockSpec((tm, k1), lambda i, j: (i, 0)), pl.BlockSpec((tm, k2), lambda i, j: (i, 0)),
                  pl.BlockSpec((k1 + k2, tn), lambda i, j: (0, j))],
        out_specs=pl.BlockSpec((tm, tn), lambda i, j: (i, j)),
        out_shape=jax.ShapeDtypeStruct((t, n), F32),
        compiler_params=_cp("parallel", "arbitrary"),
        name="mm2",
    )(x1, x2, w)


def _mm_glu_kernel(zb_ref, w_ref, z_ref, o_ref):
    s = _dot(zb_ref[...], w_ref[...].astype(BF16))
    o_ref[...] = (z_ref[...] * jax.nn.sigmoid(s)).astype(o_ref.dtype)


def _mm_glu(zb, w, z):
    t, k = zb.shape
    n = w.shape[1]
    tm, tn = _mm_specs(t, k, n)
    return pl.pallas_call(
        _mm_glu_kernel,
        grid=(t // tm, n // tn),
        in_specs=[pl.BlockSpec((tm, k), lambda i, j: (i, 0)), pl.BlockSpec((k, tn), lambda i, j: (0, j)),
                  pl.BlockSpec((tm, tn), lambda i, j: (i, j))],
        out_specs=pl.BlockSpec((tm, tn), lambda i, j: (i, j)),
        out_shape=jax.ShapeDtypeStruct((t, n), BF16),
        compiler_params=_cp("parallel", "arbitrary"),
        name="mm_glu",
    )(zb, w, z)


def _pool_prompt_kernel(u_ref, wp_ref, ps_ref, o_ref):
    s_len = u_ref.shape[0]
    pg = wp_ref.shape[1]
    row = lax.broadcasted_iota(jnp.int32, (s_len, pg), 0)
    for gi, w in enumerate(POOL_WINDOWS):
        cols = slice(gi * pg, (gi + 1) * pg)
        u = u_ref[:, cols]
        s = u
        d = 1
        while d < w:
            s = s + jnp.where(row >= d, pltpu.roll(s, d, axis=0), 0.0)
            d *= 2
        cnt = jnp.minimum(row + 1, w).astype(F32)
        dd = (s / cnt - u).astype(BF16)
        o_ref[:, cols] = (_dot(dd, wp_ref[gi]) * ps_ref[:, cols]).astype(o_ref.dtype)


def _pool_prompt(u_pool, w_pool, pool_scale, batch, seq):
    c = u_pool.shape[1]
    return pl.pallas_call(
        _pool_prompt_kernel,
        grid=(batch,),
        in_specs=[pl.BlockSpec((seq, c), lambda b: (b, 0)),
                  pl.BlockSpec(w_pool.shape, lambda b: (0, 0, 0)),
                  pl.BlockSpec((1, c), lambda b: (0, 0))],
        out_specs=pl.BlockSpec((seq, c), lambda b: (b, 0)),
        out_shape=jax.ShapeDtypeStruct((batch * seq, c), BF16),
        compiler_params=_cp("parallel"),
        name="pool_prompt",
    )(u_pool, w_pool, pool_scale.reshape(1, c))


def _pool_sample_kernel(ext_ref, wp_ref, ps_ref, o_ref, *, start):
    nb = ext_ref.shape[0]
    pg = wp_ref.shape[1]
    for gi, w in enumerate(POOL_WINDOWS):
        cols = slice(gi * pg, (gi + 1) * pg)
        u = ext_ref[nb - 1, :, cols]
        s = u
        for i in range(1, w):
            s = s + ext_ref[nb - 1 - i, :, cols]
        dd = (s / float(min(start + 1, w)) - u).astype(BF16)
        o_ref[:, cols] = (_dot(dd, wp_ref[gi]) * ps_ref[:, cols]).astype(o_ref.dtype)


def _pool_sample(ext_t, w_pool, pool_scale, start):
    nb, bd, c = ext_t.shape
    return pl.pallas_call(
        functools.partial(_pool_sample_kernel, start=start),
        grid=(1,),
        in_specs=[pl.BlockSpec((nb, bd, c), lambda i: (0, 0, 0)),
                  pl.BlockSpec(w_pool.shape, lambda i: (0, 0, 0)),
                  pl.BlockSpec((1, c), lambda i: (0, 0))],
        out_specs=pl.BlockSpec((bd, c), lambda i: (0, 0)),
        out_shape=jax.ShapeDtypeStruct((bd, c), BF16),
        compiler_params=_cp("arbitrary"),
        name="pool_sample",
    )(ext_t, w_pool, pool_scale.reshape(1, c))


def _mla_prep_kernel(h_ref, wp_ref, wq_ref, wkv_ref, wkp_ref, wkps_ref, qn_ref, kvn_ref, cos_ref, sin_ref,
                     up_ref, cq_ref, ckv_ref, ckvb_ref, kpe_ref, kpeb_ref):
    h = h_ref[...]
    up_ref[...] = _dot(h, wp_ref[...].astype(BF16))
    cq_ref[...] = _rms(_dot(h, wq_ref[...].astype(BF16)), qn_ref[...]).astype(cq_ref.dtype)
    ckv = _rms(_dot(h, wkv_ref[...].astype(BF16)), kvn_ref[...])
    ckv_ref[...] = ckv
    ckvb_ref[...] = ckv.astype(ckvb_ref.dtype)
    kpe = _dot(h, wkp_ref[...]) * cos_ref[...] + _dot(h, wkps_ref[...]) * sin_ref[...]
    kpe_ref[...] = kpe
    kpeb_ref[...] = kpe.astype(kpeb_ref.dtype)


def _mla_prep(h, w_in0, o_q, o_kv, w_kpe, w_kpe_sw, q_norm, kv_norm, cosf, sinf):
    t, d = h.shape
    ql, kl, r = q_norm.shape[0], kv_norm.shape[0], w_kpe.shape[1]
    pw = o_q
    assert o_q % ql == 0 and o_kv % kl == 0
    tm = _tile(t, (208, 128, 64, 32, 16))
    row = lambda n: pl.BlockSpec((tm, n), lambda i: (i, 0))
    full = lambda a, b: pl.BlockSpec((a, b), lambda i: (0, 0))
    win = lambda n, off: pl.BlockSpec((d, n), lambda i: (0, off // n), pipeline_mode=pl.Buffered(1))
    return pl.pallas_call(
        _mla_prep_kernel,
        grid=(t // tm,),
        in_specs=[row(d), win(pw, 0), win(ql, o_q), win(kl, o_kv), full(d, r), full(d, r), full(1, ql), full(1, kl),
                  row(r), row(r)],
        out_specs=[row(pw), row(ql), row(kl), row(kl), row(r), row(r)],
        out_shape=[jax.ShapeDtypeStruct((t, pw), F32), jax.ShapeDtypeStruct((t, ql), BF16),
                   jax.ShapeDtypeStruct((t, kl), F32), jax.ShapeDtypeStruct((t, kl), BF16),
                   jax.ShapeDtypeStruct((t, r), F32), jax.ShapeDtypeStruct((t, r), BF16)],
        compiler_params=_cp("parallel"),
        name="mla_prep",
    )(h, w_in0, w_in0, w_in0, w_kpe, w_kpe_sw, q_norm.reshape(1, ql), kv_norm.reshape(1, kl), cosf, sinf)


def _q_prep_kernel(cq_ref, wqn_ref, wqr_ref, wqrs_ref, wuk_ref, cos_ref, sin_ref, ql_ref, qp_ref):
    cq = cq_ref[...]
    hp, nope, _ = wuk_ref.shape
    r = qp_ref.shape[2]
    q_nope = _dot(cq, wqn_ref[0]).astype(BF16)
    q_pe = _dot(cq, wqr_ref[0]) * cos_ref[...] + _dot(cq, wqrs_ref[0]) * sin_ref[...]
    for hh in range(hp):
        ql_ref[hh] = _dot(q_nope[:, hh * nope:(hh + 1) * nope], wuk_ref[hh]).astype(ql_ref.dtype)
        qp_ref[hh] = q_pe[:, hh * r:(hh + 1) * r].astype(qp_ref.dtype)


def _group_heads(w, hp):
    heads, a, b = w.shape
    return w.reshape(heads // hp, hp, a, b).transpose(0, 2, 1, 3).reshape(heads // hp, a, hp * b)


def _q_prep(cq, w_qn, w_qr, w_qr_sw, w_uk_t, cosf, sinf):
    t, ql = cq.shape
    heads, _, nope = w_qn.shape
    r = w_qr.shape[2]
    kl = w_uk_t.shape[2]
    hp = _tile(heads, (4, 2, 1))
    tm = _tile(t, (1040, 512, 256, 128, 64, 32, 16))
    hw = lambda a, b: pl.BlockSpec((1, a, b), lambda i, h: (h, 0, 0))
    rows = lambda n: pl.BlockSpec((tm, n), lambda i, h: (i, 0))
    return pl.pallas_call(
        _q_prep_kernel,
        grid=(t // tm, heads // hp),
        in_specs=[rows(ql), hw(ql, hp * nope), hw(ql, hp * r), hw(ql, hp * r),
                  pl.BlockSpec((hp, nope, kl), lambda i, h: (h, 0, 0)), rows(hp * r), rows(hp * r)],
        out_specs=[pl.BlockSpec((hp, tm, kl), lambda i, h: (h, i, 0)),
                   pl.BlockSpec((hp, tm, r), lambda i, h: (h, i, 0))],
        out_shape=[jax.ShapeDtypeStruct((heads, t, kl), BF16), jax.ShapeDtypeStruct((heads, t, r), BF16)],
        compiler_params=_cp("parallel", "arbitrary"),
        name="q_prep",
    )(cq, _group_heads(w_qn, hp), _group_heads(w_qr, hp), _group_heads(w_qr_sw, hp), w_uk_t,
      jnp.tile(cosf, (1, hp)), jnp.tile(sinf, (1, hp)))


def _o_proj_kernel(ol_ref, wuv_ref, o_ref):
    o_ref[...] = _dot(ol_ref[0], wuv_ref[0]).astype(o_ref.dtype)


def _o_proj(o_lat, w_uv_t):
    heads, t, kl = o_lat.shape
    vh = w_uv_t.shape[2]
    tm = _tile(t, (1024, 512, 256, 128, 64, 32, 16))
    return pl.pallas_call(
        _o_proj_kernel,
        grid=(t // tm, heads),
        in_specs=[pl.BlockSpec((1, tm, kl), lambda i, h: (h, i, 0)), pl.BlockSpec((1, kl, vh), lambda i, h: (h, 0, 0))],
        out_specs=pl.BlockSpec((tm, vh), lambda i, h: (i, h)),
        out_shape=jax.ShapeDtypeStruct((t, heads * vh), BF16),
        compiler_params=_cp("parallel", "arbitrary"),
        name="o_proj",
    )(o_lat, w_uv_t)


def _softmax_update(s, v, m_ref, l_ref, acc_ref):
    m_prev = m_ref[...]
    m_new = jnp.maximum(m_prev, jnp.max(s, axis=1, keepdims=True))
    alpha = jnp.exp(m_prev - m_new)
    p = jnp.exp(s - _rep(m_new, s.shape[1] // LANES))
    l_ref[...] = alpha * l_ref[...] + jnp.sum(p, axis=1, keepdims=True)
    acc_ref[...] = acc_ref[...] * _rep(alpha, acc_ref.shape[1] // LANES) + _dot(p.astype(BF16), v)
    m_ref[...] = m_new


def _attn_prompt_kernel(ql_ref, qp_ref, k_ref, kp_ref, o_ref, m_ref, l_ref, acc_ref, *, scale):
    qi = pl.program_id(1)
    hg, tq, c = ql_ref.shape
    r = hg * tq
    q = ql_ref[...].reshape(r, c)
    qp = qp_ref[...].reshape(r, qp_ref.shape[2])
    m_ref[...] = jnp.full(m_ref.shape, NEG_INF, F32)
    l_ref[...] = jnp.zeros(l_ref.shape, F32)
    acc_ref[...] = jnp.zeros(acc_ref.shape, F32)

    def block(start, width, masked):
        k = k_ref[pl.ds(start, width), :]
        kp = kp_ref[pl.ds(start, width), :]
        s = (_dot_nt(q, k) + _dot_nt(qp, kp)) * scale
        if masked:
            s3 = s.reshape(hg, tq, width)
            qpos = qi * tq + lax.broadcasted_iota(jnp.int32, s3.shape, 1)
            kpos = start + lax.broadcasted_iota(jnp.int32, s3.shape, 2)
            s = jnp.where(kpos <= qpos, s3, NEG_INF).reshape(r, width)
        _softmax_update(s, k, m_ref, l_ref, acc_ref)

    def body(kb, carry):
        block(pl.multiple_of(kb * (2 * tq), 2 * tq), 2 * tq, False)
        return carry

    lax.fori_loop(0, qi // 2, body, 0)

    @pl.when(qi % 2 == 0)
    def _():
        block(pl.multiple_of(qi * tq, tq), tq, True)

    @pl.when(qi % 2 == 1)
    def _():
        block(pl.multiple_of((qi - 1) * tq, 2 * tq), 2 * tq, True)

    o = acc_ref[...] / _rep(l_ref[...], c // LANES)
    o_ref[...] = o.astype(o_ref.dtype).reshape(hg, tq, c)


def _attn_prompt(q_lat, q_pe, ckv_b, kpe_b, batch, seq, scale):
    heads, _, c = q_lat.shape
    r = q_pe.shape[2]
    tq = _tile(seq, (256, 128))
    hg = _tile(heads, (12, 8, 4, 2, 1))
    nq = seq // tq
    rows = hg * tq
    return pl.pallas_call(
        functools.partial(_attn_prompt_kernel, scale=scale),
        grid=(batch, nq, heads // hg),
        in_specs=[pl.BlockSpec((hg, tq, c), lambda b, i, g: (g, b * nq + i, 0)),
                  pl.BlockSpec((hg, tq, r), lambda b, i, g: (g, b * nq + i, 0)),
                  pl.BlockSpec((seq, c), lambda b, i, g: (b, 0)),
                  pl.BlockSpec((seq, r), lambda b, i, g: (b, 0))],
        out_specs=pl.BlockSpec((hg, tq, c), lambda b, i, g: (g, b * nq + i, 0)),
        out_shape=jax.ShapeDtypeStruct((heads, batch * seq, c), BF16),
        scratch_shapes=[pltpu.VMEM((rows, LANES), F32), pltpu.VMEM((rows, LANES), F32),
                        pltpu.VMEM((rows, c), F32)],
        compiler_params=_cp("parallel", "parallel", "arbitrary"),
        name="attn_prompt",
    )(q_lat, q_pe, ckv_b, kpe_b)


def _attn_sample_kernel(pt_ref, ql_ref, qp_ref, cn_ref, pn_ref, *rest, pps, scale):
    del pt_ref
    ck_refs, kp_refs = rest[:pps], rest[pps:2 * pps]
    o_ref, m_ref, l_ref, acc_ref = rest[2 * pps:]
    step = pl.program_id(1)
    q = ql_ref[0]
    qp = qp_ref[0]

    @pl.when(step == 0)
    def _():
        cn = cn_ref[0]
        pn = pn_ref[0]
        s0 = (jnp.sum(q.astype(F32) * cn, axis=1, keepdims=True)
              + jnp.sum(qp.astype(F32) * pn, axis=1, keepdims=True)) * scale
        m_ref[...] = jnp.broadcast_to(s0, m_ref.shape)
        l_ref[...] = jnp.ones(l_ref.shape, F32)
        acc_ref[...] = jnp.broadcast_to(cn, acc_ref.shape)

    k = jnp.concatenate([ref[0].astype(BF16) for ref in ck_refs], axis=0)
    kp_t = jnp.concatenate([ref[0].astype(BF16) for ref in kp_refs], axis=1)
    s = (_dot_nt(q, k) + _dot(qp, kp_t)) * scale
    _softmax_update(s, k, m_ref, l_ref, acc_ref)

    @pl.when(step == pl.num_programs(1) - 1)
    def _():
        o_ref[0] = (acc_ref[...] / _rep(l_ref[...], acc_ref.shape[1] // LANES)).astype(o_ref.dtype)


def _attn_sample(q_lat, q_pe, ckv_new, kpe_new, cache_ckv, cache_kpe_t, page_table, scale):
    bd, heads, c = q_lat.shape
    r = q_pe.shape[2]
    n_pages = page_table.shape[1]
    page = cache_ckv.shape[1]
    pps = _tile(n_pages, (32, 16, 8, 4, 2, 1))

    def page_map(i, b, s, pt):
        return (pt[b * n_pages + s * pps + i], 0, 0)

    per_seq = lambda n, w: pl.BlockSpec((1, n, w), lambda b, s, pt: (b, 0, 0))
    in_specs = [per_seq(heads, c), per_seq(heads, r), per_seq(1, c), per_seq(1, r)]
    in_specs += [pl.BlockSpec((1, page, c), functools.partial(page_map, i)) for i in range(pps)]
    in_specs += [pl.BlockSpec((1, r, page), functools.partial(page_map, i)) for i in range(pps)]
    return pl.pallas_call(
        functools.partial(_attn_sample_kernel, pps=pps, scale=scale),
        grid_spec=pltpu.PrefetchScalarGridSpec(
            num_scalar_prefetch=1,
            grid=(bd, n_pages // pps),
            in_specs=in_specs,
            out_specs=per_seq(heads, c),
            scratch_shapes=[pltpu.VMEM((heads, LANES), F32), pltpu.VMEM((heads, LANES), F32),
                            pltpu.VMEM((heads, c), F32)],
        ),
        out_shape=jax.ShapeDtypeStruct((bd, heads, c), BF16),
        compiler_params=_cp("parallel", "arbitrary"),
        name="attn_sample",
    )(page_table.reshape(-1), q_lat, q_pe, ckv_new, kpe_new, *([cache_ckv] * pps), *([cache_kpe_t] * pps))


def _s5_disc_kernel(lr_ref, li_ref, ldt_ref, br_ref, bi_ref, bbr_ref, bbi_ref, ar_ref, ai_ref):
    dt = jnp.exp(ldt_ref[...])
    lr = lr_ref[...]
    li = li_ref[...]
    mag = jnp.exp(lr * dt)
    a_re = mag * jnp.cos(li * dt)
    a_im = mag * jnp.sin(li * dt)
    den = lr * lr + li * li
    nr, ni = a_re - 1.0, a_im
    f_re = (nr * lr + ni * li) / den
    f_im = (ni * lr - nr * li) / den
    for k in range(br_ref.shape[0]):
        bbr_ref[k] = f_re * br_ref[k] - f_im * bi_ref[k]
        bbi_ref[k] = f_re * bi_ref[k] + f_im * br_ref[k]
    ar_ref[...] = a_re
    ai_ref[...] = a_im


def _s5_disc(lam_re, lam_im, log_dt, b_re_t, b_im_t):
    k, g, n = b_re_t.shape
    spec2 = pl.BlockSpec((g, n), lambda i: (0, 0))
    spec3 = lambda a: pl.BlockSpec((a, g, n), lambda i: (0, 0, 0))
    return pl.pallas_call(
        _s5_disc_kernel,
        grid=(1,),
        in_specs=[spec2, spec2, spec2, spec3(k), spec3(k)],
        out_specs=[spec3(k), spec3(k), spec2, spec2],
        out_shape=[jax.ShapeDtypeStruct((k, g, n), F32)] * 2 + [jax.ShapeDtypeStruct((g, n), F32)] * 2,
        compiler_params=_cp("arbitrary"),
        name="s5_disc",
    )(lam_re, lam_im, jnp.broadcast_to(log_dt[:, None], (g, n)), b_re_t, b_im_t)


def _s5_tail(h_re, h_im, u, cr_ref, ci_ref, d_ref, z_ref, zb_ref):
    y = _dot(h_re.astype(BF16), cr_ref[0]) - _dot(h_im.astype(BF16), ci_ref[0]) + d_ref[...] * u
    z = jax.nn.gelu(y, approximate=True)
    z_ref[...] = z
    zb_ref[...] = z.astype(zb_ref.dtype)


def _s5_scan_kernel(u_ref, bbr_ref, bbi_ref, pr_ref, pi_ref, cr_ref, ci_ref, d_ref, h0r_ref, h0i_ref,
                    z_ref, zb_ref, hr_out_ref, hi_out_ref,
                    st_ref, up_ref, sr_ref, si_ref, ir_ref, ii_ref, cr_carry, ci_carry):
    tc = pl.program_id(2)
    tt, w = sr_ref.shape
    nl = tt // SUBLANES
    assert nl & (nl - 1) == 0, "sub-block length must be a power of two (A_bar^nl by squaring)"

    @pl.when(tc == 0)
    def _():
        cr_carry[...] = h0r_ref[0, 0]
        ci_carry[...] = h0i_ref[0, 0]

    n_lt = u_ref.shape[1] // LANES
    pitch = st_ref.shape[1] // SUBLANES
    for s in range(SUBLANES):
        for j in range(n_lt):
            st_ref[j, s * pitch:s * pitch + nl, :] = u_ref[s * nl:(s + 1) * nl, j * LANES:(j + 1) * LANES]
    for i in range(nl):
        for j in range(n_lt):
            up_ref[i * SUBLANES:(i + 1) * SUBLANES, j * LANES:(j + 1) * LANES] = (
                st_ref[j, pl.ds(i, SUBLANES, stride=pitch), :])
    ub = up_ref[...].astype(BF16)
    sr_ref[...] = _dot(ub, bbr_ref[0])
    si_ref[...] = _dot(ub, bbi_ref[0])
    a_re = pr_ref[0]
    a_im = pi_ref[0]
    a_re8 = jnp.broadcast_to(a_re, (SUBLANES, w))
    a_im8 = jnp.broadcast_to(a_im, (SUBLANES, w))

    def pass1(i, carry):
        h_re, h_im = carry
        off = pl.multiple_of(i * SUBLANES, SUBLANES)
        x_re = sr_ref[pl.ds(off, SUBLANES), :]
        x_im = si_ref[pl.ds(off, SUBLANES), :]
        h_re, h_im = a_re8 * h_re - a_im8 * h_im + x_re, a_re8 * h_im + a_im8 * h_re + x_im
        sr_ref[pl.ds(off, SUBLANES), :] = h_re
        si_ref[pl.ds(off, SUBLANES), :] = h_im
        return h_re, h_im

    zero = jnp.zeros((SUBLANES, w), F32)
    f_re, f_im = lax.fori_loop(0, nl, pass1, (zero, zero))

    al_re, al_im = a_re, a_im
    n = nl
    while n > 1:
        al_re, al_im = al_re * al_re - al_im * al_im, 2.0 * (al_re * al_im)
        n //= 2
    c_re, c_im = cr_carry[...], ci_carry[...]
    for s in range(SUBLANES):
        ir_ref[s:s + 1, :] = c_re
        ii_ref[s:s + 1, :] = c_im
        c_re, c_im = (f_re[s:s + 1, :] + al_re * c_re - al_im * c_im,
                      f_im[s:s + 1, :] + al_re * c_im + al_im * c_re)
    cr_carry[...] = c_re
    ci_carry[...] = c_im
    i_re = ir_ref[...]
    i_im = ii_ref[...]

    def pass2(i, carry):
        p_re, p_im = carry
        off = pl.multiple_of(i * SUBLANES, SUBLANES)
        p_re8 = jnp.broadcast_to(p_re, (SUBLANES, w))
        p_im8 = jnp.broadcast_to(p_im, (SUBLANES, w))
        sr_ref[pl.ds(off, SUBLANES), :] = sr_ref[pl.ds(off, SUBLANES), :] + (p_re8 * i_re - p_im8 * i_im)
        si_ref[pl.ds(off, SUBLANES), :] = si_ref[pl.ds(off, SUBLANES), :] + (p_re8 * i_im + p_im8 * i_re)
        return p_re * a_re - p_im * a_im, p_re * a_im + p_im * a_re

    lax.fori_loop(0, nl, pass2, (a_re, a_im))

    y = _dot(sr_ref[...].astype(BF16), cr_ref[0]) - _dot(si_ref[...].astype(BF16), ci_ref[0])
    for i in range(nl):
        for j in range(n_lt):
            st_ref[j, pl.ds(i, SUBLANES, stride=pitch), :] = (
                y[i * SUBLANES:(i + 1) * SUBLANES, j * LANES:(j + 1) * LANES])
    for s in range(SUBLANES):
        rows = slice(s * nl, (s + 1) * nl)
        for j in range(n_lt):
            cols = slice(j * LANES, (j + 1) * LANES)
            z = jax.nn.gelu(st_ref[j, s * pitch:s * pitch + nl, :] + d_ref[:, cols] * u_ref[rows, cols],
                            approximate=True)
            z_ref[rows, cols] = z
            zb_ref[rows, cols] = z.astype(zb_ref.dtype)

    @pl.when(tc == pl.num_programs(2) - 1)
    def _():
        hr_out_ref[0, 0] = c_re
        hi_out_ref[0, 0] = c_im


def _s5_scan(u, row_off, batch, seq, bb_re, bb_im, pw_re, pw_im, cm_re, cm_im, d_skip, h0_re, h0_im):
    n_chunks, cw_in, cw_st = bb_re.shape
    tt = _tile(seq, (512, 256, 128, 64, 32, 16, 8))
    assert seq % tt == 0 and row_off % tt == 0
    nt = seq // tt
    ob = row_off // tt
    cmat = lambda a, b: pl.BlockSpec((1, a, b), lambda b_, c, t: (c, 0, 0))
    st = pl.BlockSpec((1, 1, 1, cw_st), lambda b_, c, t: (b_, c, 0, 0))
    urow = pl.BlockSpec((tt, cw_in), lambda b_, c, t: (ob + b_ * nt + t, c))
    orow = urow
    rows = u.shape[0]
    wd = n_chunks * cw_in
    return pl.pallas_call(
        _s5_scan_kernel,
        grid=(batch, n_chunks, nt),
        in_specs=[urow, cmat(cw_in, cw_st), cmat(cw_in, cw_st), cmat(1, cw_st), cmat(1, cw_st),
                  cmat(cw_st, cw_in), cmat(cw_st, cw_in), pl.BlockSpec((1, cw_in), lambda b_, c, t: (0, c)), st, st],
        out_specs=[orow, orow, st, st],
        out_shape=[jax.ShapeDtypeStruct((rows, wd), F32), jax.ShapeDtypeStruct((rows, wd), BF16),
                   jax.ShapeDtypeStruct(h0_re.shape, F32), jax.ShapeDtypeStruct(h0_im.shape, F32)],
        scratch_shapes=[pltpu.VMEM((cw_in // LANES, tt + SUBLANES * SUBLANES, LANES), F32),
                        pltpu.VMEM((tt, cw_in), F32),
                        pltpu.VMEM((tt, cw_st), F32), pltpu.VMEM((tt, cw_st), F32),
                        pltpu.VMEM((SUBLANES, cw_st), F32), pltpu.VMEM((SUBLANES, cw_st), F32),
                        pltpu.VMEM((1, cw_st), F32), pltpu.VMEM((1, cw_st), F32)],
        compiler_params=_cp("parallel", "parallel", "arbitrary"),
        name="s5_scan",
    )(u, bb_re, bb_im, pw_re, pw_im, cm_re, cm_im, d_skip.reshape(1, wd), h0_re, h0_im)


def _s5_step_kernel(u_ref, bbr_ref, bbi_ref, pr_ref, pi_ref, cr_ref, ci_ref, d_ref, h0r_ref, h0i_ref,
                    z_in_ref, zb_in_ref, z_ref, zb_ref, hr_out_ref, hi_out_ref):
    del z_in_ref, zb_in_ref
    u = u_ref[...]
    ub = u.astype(BF16)
    a_re = pr_ref[0]
    a_im = pi_ref[0]
    h0r = h0r_ref[...]
    h0i = h0i_ref[...]
    h_re = _dot(ub, bbr_ref[0]) + (a_re * h0r - a_im * h0i)
    h_im = _dot(ub, bbi_ref[0]) + (a_re * h0i + a_im * h0r)
    hr_out_ref[...] = h_re
    hi_out_ref[...] = h_im
    _s5_tail(h_re, h_im, u, cr_ref, ci_ref, d_ref, z_ref, zb_ref)


def _s5_step(u, row_off, n_seq, bb_re, bb_im, pw_re, pw_im, cm_re, cm_im, d_skip, h0_re, h0_im, z, zb):
    n_chunks, cw_in, cw_st = bb_re.shape
    ob = row_off // n_seq
    cmat = lambda a, b: pl.BlockSpec((1, a, b), lambda c: (c, 0, 0))
    st = pl.BlockSpec((n_seq, cw_st), lambda c: (0, c))
    urow = pl.BlockSpec((n_seq, cw_in), lambda c: (ob, c))
    wd = n_chunks * cw_in
    anyspec = pl.BlockSpec(memory_space=pl.ANY)
    return pl.pallas_call(
        _s5_step_kernel,
        grid=(n_chunks,),
        in_specs=[urow, cmat(cw_in, cw_st), cmat(cw_in, cw_st),
                  cmat(1, cw_st), cmat(1, cw_st), cmat(cw_st, cw_in), cmat(cw_st, cw_in),
                  pl.BlockSpec((1, cw_in), lambda c: (0, c)), st, st, anyspec, anyspec],
        out_specs=[urow, urow, st, st],
        out_shape=[jax.ShapeDtypeStruct(z.shape, z.dtype), jax.ShapeDtypeStruct(zb.shape, zb.dtype),
                   jax.ShapeDtypeStruct(h0_re.shape, F32), jax.ShapeDtypeStruct(h0_im.shape, F32)],
        input_output_aliases={10: 0, 11: 1},
        compiler_params=_cp("parallel"),
        name="s5_step",
    )(u, bb_re, bb_im, pw_re, pw_im, cm_re, cm_im, d_skip.reshape(1, wd), h0_re, h0_im, z, zb)


def _block_diag(m, gpc):
    g, a, b = m.shape
    eye = jnp.eye(gpc, dtype=m.dtype)
    mc = m.reshape(g // gpc, gpc, a, b)
    blk = mc[:, :, :, None, :] * eye[None, :, None, :, None]
    return blk.reshape(g // gpc, gpc * a, gpc * b)


def kernel(x_prompt, x_sample, cache_ckv, cache_kpe, page_table, state_pool, state_ssm_re, state_ssm_im, norm_gains, w_ffn_gate, w_ffn_up, w_ffn_down, w_in0, w_pool, pool_scale, q_norm, kv_norm, w_uq, w_uk, w_uv, w_out0, w_in1, lam_re, lam_im, log_dt, b_re, b_im, c_re, c_im, d_skip, w_glu, w_out1):
    batch, seq, d_model = x_prompt.shape
    bd, dec_seq, _ = x_sample.shape
    assert dec_seq == 1, "the sample group carries one new token per sequence"
    n_prompt = batch * seq
    n_tok = n_prompt + bd
    pool_buf, pool_width = state_pool.shape[1], state_pool.shape[2]
    assert seq >= pool_buf and n_prompt % bd == 0
    kv_lora, heads, qk_nope = w_uk.shape
    v_head = w_uv.shape[2]
    q_lora = q_norm.shape[0]
    qk_rope = cache_kpe.shape[2]
    half = qk_rope // 2
    past_len = page_table.shape[1] * cache_ckv.shape[1]
    sm_scale = float(qk_nope + qk_rope) ** -0.5
    groups, n_state, grp = b_re.shape
    depth = norm_gains.shape[0]
    assert depth == 2

    wg, wu, wd = w_ffn_gate, w_ffn_up, w_ffn_down
    o_q, o_kv, o_pe = pool_width, pool_width + q_lora, pool_width + q_lora + kv_lora
    swap = jnp.concatenate([jnp.arange(half, qk_rope), jnp.arange(half)])
    w_kpe = w_in0[:, o_pe:].astype(BF16)
    w_kpe_sw = w_kpe[:, swap]
    w_uq3 = w_uq.astype(BF16).reshape(q_lora, heads, qk_nope + qk_rope).transpose(1, 0, 2)
    w_qn, w_qr = w_uq3[:, :, :qk_nope], w_uq3[:, :, qk_nope:]
    w_qr_sw = w_qr[:, :, swap]
    w_uk_t = w_uk.astype(BF16).transpose(1, 2, 0)
    w_uv_t = w_uv.astype(BF16).transpose(1, 0, 2)
    w_poolb = w_pool.astype(BF16)
    cache_kpe_t = cache_kpe.transpose(0, 2, 1)

    inv_freq = ROPE_THETA ** (-jnp.arange(half, dtype=F32) / half)
    pos = jnp.concatenate([jnp.tile(jnp.arange(seq), batch), jnp.full((bd,), past_len)]).astype(F32)
    ang = pos[:, None] * inv_freq[None, :]
    cos, sin = jnp.cos(ang), jnp.sin(ang)
    cosf = jnp.concatenate([cos, cos], axis=1)
    sinf = jnp.concatenate([-sin, sin], axis=1)

    x = jnp.concatenate([x_prompt.reshape(n_prompt, d_model), x_sample.reshape(bd, d_model)], axis=0)

    g = norm_gains[0]
    h = _prenorm(x, g[0])
    x, h = _post(x, _ffn(h, wg, wu, wd, 0, 0), g[1], 0.5, g[2])

    u_pool, cq, ckv, ckv_b, kpe, kpe_b = _mla_prep(h, w_in0, o_q, o_kv, w_kpe, w_kpe_sw, q_norm, kv_norm,
                                                   cosf, sinf)
    y_pool_p = _pool_prompt(u_pool, w_poolb, pool_scale, batch, seq)
    ext_t = jnp.concatenate([state_pool.transpose(1, 0, 2), u_pool[None, n_prompt:]], axis=0)
    y_pool_s = _pool_sample(ext_t, w_poolb, pool_scale, past_len)
    y_pool = jnp.concatenate([y_pool_p, y_pool_s], axis=0)

    q_lat, q_pe = _q_prep(cq, w_qn, w_qr, w_qr_sw, w_uk_t, cosf, sinf)
    o_lat_p = _attn_prompt(q_lat, q_pe, ckv_b, kpe_b, batch, seq, sm_scale)
    o_lat_s = _attn_sample(q_lat[:, n_prompt:].transpose(1, 0, 2), q_pe[:, n_prompt:].transpose(1, 0, 2),
                           ckv[n_prompt:, None, :], kpe[n_prompt:, None, :],
                           cache_ckv, cache_kpe_t, page_table, sm_scale)
    o = jnp.concatenate([_o_proj(o_lat_p, w_uv_t), _o_proj(o_lat_s.transpose(1, 0, 2), w_uv_t)], axis=0)
    y = _mm2(y_pool, o, w_out0)

    x, h = _post(x, y, g[3], 1.0, g[4])
    g1 = norm_gains[1]
    x, h = _post(x, _ffn(h, wg, wu, wd, 0, 1), g[5], 0.5, g1[0])

    x, h = _post(x, _ffn(h, wg, wu, wd, 1, 0), g1[1], 0.5, g1[2])

    u = _mm(h, w_in1)
    gpc = _tile(groups, (S5_CHUNK_GROUPS, 8, 4, 2, 1))
    n_chunks = groups // gpc
    cw_st = gpc * n_state
    bbt_re, bbt_im, pw_re, pw_im = _s5_disc(lam_re, lam_im, log_dt, b_re.transpose(2, 0, 1), b_im.transpose(2, 0, 1))
    bb_re = _block_diag(bbt_re.transpose(1, 0, 2), gpc).astype(BF16)
    bb_im = _block_diag(bbt_im.transpose(1, 0, 2), gpc).astype(BF16)
    cm_re = _block_diag(c_re.transpose(0, 2, 1), gpc).astype(BF16)
    cm_im = _block_diag(c_im.transpose(0, 2, 1), gpc).astype(BF16)
    pw_re = pw_re.reshape(n_chunks, 1, cw_st)
    pw_im = pw_im.reshape(n_chunks, 1, cw_st)
    zero_h = jnp.zeros((batch, n_chunks, 1, cw_st), F32)
    z, zb, hr_p, hi_p = _s5_scan(u, 0, batch, seq, bb_re, bb_im, pw_re, pw_im, cm_re, cm_im, d_skip,
                                 zero_h, zero_h)
    z, zb, hr_s, hi_s = _s5_step(u, n_prompt, bd, bb_re, bb_im, pw_re, pw_im, cm_re, cm_im, d_skip,
                                 state_ssm_re.reshape(bd, groups * n_state),
                                 state_ssm_im.reshape(bd, groups * n_state), z, zb)
    y = _mm(_mm_glu(zb, w_glu, z), w_out1)

    x, h = _post(x, y, g1[3], 1.0, g1[4])
    xp, xs = _post_last(x, _ffn(h, wg, wu, wd, 1, 1), g1[5], 0.5, n_prompt)

    y_prompt = xp.reshape(batch, seq, d_model)
    y_sample = xs.reshape(bd, 1, d_model)
    pool_p = u_pool[:n_prompt].reshape(batch, seq, pool_width)[:, seq - pool_buf:]
    pool_s = jnp.concatenate([state_pool[:, 1:], u_pool[n_prompt:, None, :]], axis=1)
    ckv_p = ckv[:n_prompt].reshape(batch, seq, kv_lora)
    ckv_s = ckv[n_prompt:].reshape(bd, 1, kv_lora)
    kpe_p = kpe[:n_prompt].reshape(batch, seq, qk_rope)
    kpe_s = kpe[n_prompt:].reshape(bd, 1, qk_rope)
    return (y_prompt, y_sample, pool_p, pool_s, ckv_p, ckv_s, kpe_p, kpe_s,
            hr_p.reshape(batch, groups, n_state), hi_p.reshape(batch, groups, n_state),
            hr_s.reshape(bd, groups, n_state), hi_s.reshape(bd, groups, n_state))
```

```python
import functools

import jax
import jax.numpy as jnp
from jax import lax
from jax.experimental import pallas as pl
from jax.experimental.pallas import tpu as pltpu

F32 = jnp.float32
BF16 = jnp.bfloat16

EPS = 1e-6
ROPE_THETA = 10000.0
POOL_WINDOWS = (2, 4, 8, 16)
NEG_INF = -1e30

V7X_VMEM_LIMIT_BYTES = 56 * 1024 * 1024
FFN_VMEM_LIMIT_BYTES = 58 * 1024 * 1024
LANES = 128
SUBLANES = 8
S5_CHUNK_GROUPS = 16


def _tile(n, prefs):
    for p in prefs:
        if n % p == 0:
            return p
    return n


def _cp(*sem):
    return pltpu.CompilerParams(dimension_semantics=sem, vmem_limit_bytes=V7X_VMEM_LIMIT_BYTES)


def _rms(x, g):
    return x * lax.rsqrt(jnp.mean(x * x, axis=-1, keepdims=True) + EPS) * g


def _dot(a, b):
    return jnp.dot(a, b, preferred_element_type=F32)


def _dot_nt(a, b):
    return lax.dot_general(a, b, (((1,), (1,)), ((), ())), preferred_element_type=F32)


def _rep(x, n):
    return x if n == 1 else jnp.concatenate([x] * n, axis=1)


def _prenorm_kernel(xp_ref, xs_ref, g_ref, x_ref, h_ref, *, n_prompt_tiles):
    i = pl.program_id(0)

    def emit(src_ref):
        x = src_ref[...]
        x_ref[...] = x
        h_ref[...] = _rms(x, g_ref[...]).astype(h_ref.dtype)

    @pl.when(i < n_prompt_tiles)
    def _():
        emit(xp_ref)

    @pl.when(i >= n_prompt_tiles)
    def _():
        emit(xs_ref)


def _prenorm(x_prompt, x_sample, g):
    n_prompt, d = x_prompt.shape
    tm = x_sample.shape[0]
    assert n_prompt % tm == 0 and tm % (2 * SUBLANES) == 0
    npt = n_prompt // tm
    t = n_prompt + tm
    row = pl.BlockSpec((tm, d), lambda i: (i, 0))
    return pl.pallas_call(
        functools.partial(_prenorm_kernel, n_prompt_tiles=npt),
        grid=(npt + 1,),
        in_specs=[pl.BlockSpec((tm, d), lambda i: (jnp.minimum(i, npt - 1), 0)),
                  pl.BlockSpec((tm, d), lambda i: (0, 0)), pl.BlockSpec((1, d), lambda i: (0, 0))],
        out_specs=[row, row],
        out_shape=[jax.ShapeDtypeStruct((t, d), F32), jax.ShapeDtypeStruct((t, d), BF16)],
        compiler_params=_cp("parallel"),
        name="prenorm",
    )(x_prompt, x_sample, g.reshape(1, d))


def _post_kernel(x_ref, y_ref, gp_ref, gn_ref, xo_ref, ho_ref, *, scale):
    xn = x_ref[...] + scale * _rms(y_ref[...], gp_ref[...])
    xo_ref[...] = xn
    ho_ref[...] = _rms(xn, gn_ref[...]).astype(ho_ref.dtype)


def _post_last_kernel(x_ref, y_ref, gp_ref, xp_ref, xs_ref, *, scale, n_prompt_tiles):
    i = pl.program_id(0)
    xn = x_ref[...] + scale * _rms(y_ref[...], gp_ref[...])

    @pl.when(i < n_prompt_tiles)
    def _():
        xp_ref[...] = xn

    @pl.when(i >= n_prompt_tiles)
    def _():
        xs_ref[...] = xn


def _post_last(x, y, g_post, scale, n_prompt):
    t, d = x.shape
    tm = t - n_prompt
    assert n_prompt % tm == 0 and tm % SUBLANES == 0
    npt = n_prompt // tm
    row = pl.BlockSpec((tm, d), lambda i: (i, 0))
    vec = pl.BlockSpec((1, d), lambda i: (0, 0))
    return pl.pallas_call(
        functools.partial(_post_last_kernel, scale=scale, n_prompt_tiles=npt),
        grid=(t // tm,),
        in_specs=[row, row, vec],
        out_specs=[pl.BlockSpec((tm, d), lambda i: (jnp.minimum(i, npt - 1), 0)),
                   pl.BlockSpec((tm, d), lambda i: (0, 0))],
        out_shape=[jax.ShapeDtypeStruct((n_prompt, d), F32), jax.ShapeDtypeStruct((tm, d), F32)],
        compiler_params=_cp("arbitrary"),
        name="post_last",
    )(x, y, g_post.reshape(1, d))


def _post(x, y, g_post, scale, g_next):
    t, d = x.shape
    tm = _tile(t, (208, 128, 64, 32, 16))
    row = pl.BlockSpec((tm, d), lambda i: (i, 0))
    vec = pl.BlockSpec((1, d), lambda i: (0, 0))
    return pl.pallas_call(
        functools.partial(_post_kernel, scale=scale),
        grid=(t // tm,),
        in_specs=[row, row, vec, vec],
        out_specs=[row, row],
        out_shape=[jax.ShapeDtypeStruct((t, d), F32), jax.ShapeDtypeStruct((t, d), BF16)],
        compiler_params=_cp("parallel"),
        name="post",
    )(x, y, g_post.reshape(1, d), g_next.reshape(1, d))


def _ffn_kernel(h_ref, wg_ref, wu_ref, wd_ref, o_ref, *, n_chunks):
    j = pl.program_id(1)

    @pl.when(j == 0)
    def _():
        o_ref[...] = jnp.zeros(o_ref.shape, o_ref.dtype)

    h = h_ref[...]
    g = _dot(h, wg_ref[...].astype(BF16))
    u = _dot(h, wu_ref[...].astype(BF16))
    a = (g * jax.nn.sigmoid(g) * u).astype(BF16)
    cw = o_ref.shape[1] // n_chunks
    for c in range(n_chunks):
        o_ref[:, c * cw:(c + 1) * cw] += _dot(a, wd_ref[:, c * cw:(c + 1) * cw].astype(BF16))


def _ffn(h, wg, wu, wd, layer, half):
    t, d = h.shape
    f = wg.shape[-1]
    tm = _tile(t, (1040, 512, 256, 128, 64, 32, 16))
    tf = _tile(f, (256, 128))
    n_chunks = max(1, d // 512)
    return pl.pallas_call(
        functools.partial(_ffn_kernel, n_chunks=n_chunks),
        grid=(t // tm, f // tf),
        in_specs=[
            pl.BlockSpec((tm, d), lambda i, j: (i, 0), pipeline_mode=pl.Buffered(1)),
            pl.BlockSpec((None, None, d, tf), lambda i, j: (layer, half, 0, j)),
            pl.BlockSpec((None, None, d, tf), lambda i, j: (layer, half, 0, j)),
            pl.BlockSpec((None, None, tf, d), lambda i, j: (layer, half, j, 0)),
        ],
        out_specs=pl.BlockSpec((tm, d), lambda i, j: (i, 0), pipeline_mode=pl.Buffered(1)),
        out_shape=jax.ShapeDtypeStruct((t, d), F32),
        compiler_params=pltpu.CompilerParams(dimension_semantics=("parallel", "arbitrary"),
                                             vmem_limit_bytes=FFN_VMEM_LIMIT_BYTES),
        name="ffn",
    )(h, wg, wu, wd)


def _mm_kernel(x_ref, w_ref, o_ref):
    o_ref[...] = _dot(x_ref[...], w_ref[...].astype(BF16)).astype(o_ref.dtype)


def _mm_specs(t, k, n):
    tm = _tile(t, (1040, 512, 256, 128, 64, 32, 16))
    tn = _tile(n, (512, 256, 128))
    return tm, tn


def _mm(x, w):
    t, k = x.shape
    n = w.shape[1]
    tm, tn = _mm_specs(t, k, n)
    return pl.pallas_call(
        _mm_kernel,
        grid=(t // tm, n // tn),
        in_specs=[pl.BlockSpec((tm, k), lambda i, j: (i, 0)), pl.BlockSpec((k, tn), lambda i, j: (0, j))],
        out_specs=pl.BlockSpec((tm, tn), lambda i, j: (i, j)),
        out_shape=jax.ShapeDtypeStruct((t, n), F32),
        compiler_params=_cp("parallel", "arbitrary"),
        name="mm",
    )(x, w)


def _mm2_kernel(x1_ref, x2_ref, w_ref, o_ref):
    k1 = x1_ref.shape[1]
    o_ref[...] = (_dot(x1_ref[...], w_ref[:k1, :].astype(BF16))
                  + _dot(x2_ref[...], w_ref[k1:, :].astype(BF16)))


def _mm2(x1, x2, w):
    t, k1 = x1.shape
    k2 = x2.shape[1]
    n = w.shape[1]
    tm, tn = _mm_specs(t, k1 + k2, n)
    return pl.pallas_call(
        _mm2_kernel,
        grid=(t // tm, n // tn),
        in_specs=[pl.BlockSpec((tm, k1), lambda i, j: (i, 0)), pl.BlockSpec((tm, k2), lambda i, j: (i, 0)),
                  pl.BlockSpec((k1 + k2, tn), lambda i, j: (0, j))],
        out_specs=pl.BlockSpec((tm, tn), lambda i, j: (i, j)),
        out_shape=jax.ShapeDtypeStruct((t, n), F32),
        compiler_params=_cp("parallel", "arbitrary"),
        name="mm2",
    )(x1, x2, w)


def _mm_glu_kernel(zb_ref, w_ref, z_ref, o_ref):
    s = _dot(zb_ref[...], w_ref[...].astype(BF16))
    o_ref[...] = (z_ref[...] * jax.nn.sigmoid(s)).astype(o_ref.dtype)


def _mm_glu(zb, w, z):
    t, k = zb.shape
    n = w.shape[1]
    tm, tn = _mm_specs(t, k, n)
    return pl.pallas_call(
        _mm_glu_kernel,
        grid=(t // tm, n // tn),
        in_specs=[pl.BlockSpec((tm, k), lambda i, j: (i, 0)), pl.BlockSpec((k, tn), lambda i, j: (0, j)),
                  pl.BlockSpec((tm, tn), lambda i, j: (i, j))],
        out_specs=pl.BlockSpec((tm, tn), lambda i, j: (i, j)),
        out_shape=jax.ShapeDtypeStruct((t, n), BF16),
        compiler_params=_cp("parallel", "arbitrary"),
        name="mm_glu",
    )(zb, w, z)


def _pool_prompt_kernel(u_ref, wp_ref, ps_ref, o_ref):
    s_len = u_ref.shape[0]
    pg = wp_ref.shape[1]
    row = lax.broadcasted_iota(jnp.int32, (s_len, pg), 0)
    for gi, w in enumerate(POOL_WINDOWS):
        cols = slice(gi * pg, (gi + 1) * pg)
        u = u_ref[:, cols]
        s = u
        d = 1
        while d < w:
            s = s + jnp.where(row >= d, pltpu.roll(s, d, axis=0), 0.0)
            d *= 2
        cnt = jnp.minimum(row + 1, w).astype(F32)
        dd = (s / cnt - u).astype(BF16)
        o_ref[:, cols] = (_dot(dd, wp_ref[gi]) * ps_ref[:, cols]).astype(o_ref.dtype)


def _pool_prompt(u_pool, w_pool, pool_scale, batch, seq):
    c = u_pool.shape[1]
    return pl.pallas_call(
        _pool_prompt_kernel,
        grid=(batch,),
        in_specs=[pl.BlockSpec((seq, c), lambda b: (b, 0)),
                  pl.BlockSpec(w_pool.shape, lambda b: (0, 0, 0)),
                  pl.BlockSpec((1, c), lambda b: (0, 0))],
        out_specs=pl.BlockSpec((seq, c), lambda b: (b, 0)),
        out_shape=jax.ShapeDtypeStruct((u_pool.shape[0], c), BF16),
        compiler_params=_cp("parallel"),
        name="pool_prompt",
    )(u_pool, w_pool, pool_scale.reshape(1, c))


def _pool_sample_kernel(ext_ref, wp_ref, ps_ref, o_in_ref, o_ref, *, start):
    del o_in_ref
    nb = ext_ref.shape[0]
    pg = wp_ref.shape[1]
    for gi, w in enumerate(POOL_WINDOWS):
        cols = slice(gi * pg, (gi + 1) * pg)
        u = ext_ref[nb - 1, :, cols]
        s = u
        for i in range(1, w):
            s = s + ext_ref[nb - 1 - i, :, cols]
        dd = (s / float(min(start + 1, w)) - u).astype(BF16)
        o_ref[:, cols] = (_dot(dd, wp_ref[gi]) * ps_ref[:, cols]).astype(o_ref.dtype)


def _pool_sample(ext_t, w_pool, pool_scale, start, y_full, row_off):
    nb, bd, c = ext_t.shape
    assert row_off % bd == 0
    rb = row_off // bd
    return pl.pallas_call(
        functools.partial(_pool_sample_kernel, start=start),
        grid=(1,),
        in_specs=[pl.BlockSpec((nb, bd, c), lambda i: (0, 0, 0)),
                  pl.BlockSpec(w_pool.shape, lambda i: (0, 0, 0)),
                  pl.BlockSpec((1, c), lambda i: (0, 0)),
                  pl.BlockSpec(memory_space=pl.ANY)],
        out_specs=pl.BlockSpec((bd, c), lambda i: (rb, 0)),
        out_shape=jax.ShapeDtypeStruct(y_full.shape, y_full.dtype),
        input_output_aliases={3: 0},
        compiler_params=_cp("arbitrary"),
        name="pool_sample",
    )(ext_t, w_pool, pool_scale.reshape(1, c), y_full)


def _mla_prep_kernel(h_ref, wp_ref, wq_ref, wkv_ref, wkp_ref, wkps_ref, qn_ref, kvn_ref, cos_ref, sin_ref,
                     up_ref, cq_ref, ckv_ref, ckvb_ref, kpe_ref, kpeb_ref):
    h = h_ref[...]
    up_ref[...] = _dot(h, wp_ref[...].astype(BF16))
    cq_ref[...] = _rms(_dot(h, wq_ref[...].astype(BF16)), qn_ref[...]).astype(cq_ref.dtype)
    ckv = _rms(_dot(h, wkv_ref[...].astype(BF16)), kvn_ref[...])
    ckv_ref[...] = ckv
    ckvb_ref[...] = ckv.astype(ckvb_ref.dtype)
    kpe = _dot(h, wkp_ref[...]) * cos_ref[...] + _dot(h, wkps_ref[...]) * sin_ref[...]
    kpe_ref[...] = kpe
    kpeb_ref[...] = kpe.astype(kpeb_ref.dtype)


def _mla_prep(h, w_in0, o_q, o_kv, w_kpe, w_kpe_sw, q_norm, kv_norm, cosf, sinf):
    t, d = h.shape
    ql, kl, r = q_norm.shape[0], kv_norm.shape[0], w_kpe.shape[1]
    pw = o_q
    assert o_q % ql == 0 and o_kv % kl == 0
    tm = _tile(t, (208, 128, 64, 32, 16))
    row = lambda n: pl.BlockSpec((tm, n), lambda i: (i, 0))
    full = lambda a, b: pl.BlockSpec((a, b), lambda i: (0, 0))
    win = lambda n, off: pl.BlockSpec((d, n), lambda i: (0, off // n), pipeline_mode=pl.Buffered(1))
    return pl.pallas_call(
        _mla_prep_kernel,
        grid=(t // tm,),
        in_specs=[row(d), win(pw, 0), win(ql, o_q), win(kl, o_kv), full(d, r), full(d, r), full(1, ql), full(1, kl),
                  row(r), row(r)],
        out_specs=[row(pw), row(ql), row(kl), row(kl), row(r), row(r)],
        out_shape=[jax.ShapeDtypeStruct((t, pw), F32), jax.ShapeDtypeStruct((t, ql), BF16),
                   jax.ShapeDtypeStruct((t, kl), F32), jax.ShapeDtypeStruct((t, kl), BF16),
                   jax.ShapeDtypeStruct((t, r), F32), jax.ShapeDtypeStruct((t, r), BF16)],
        compiler_params=_cp("parallel"),
        name="mla_prep",
    )(h, w_in0, w_in0, w_in0, w_kpe, w_kpe_sw, q_norm.reshape(1, ql), kv_norm.reshape(1, kl), cosf, sinf)


def _q_prep_kernel(cq_ref, wqn_ref, wqr_ref, wqrs_ref, wuk_ref, cos_ref, sin_ref, ql_ref, qp_ref):
    cq = cq_ref[...]
    hp, nope, _ = wuk_ref.shape
    r = qp_ref.shape[2]
    q_nope = _dot(cq, wqn_ref[0]).astype(BF16)
    q_pe = _dot(cq, wqr_ref[0]) * cos_ref[...] + _dot(cq, wqrs_ref[0]) * sin_ref[...]
    for hh in range(hp):
        ql_ref[hh] = _dot(q_nope[:, hh * nope:(hh + 1) * nope], wuk_ref[hh]).astype(ql_ref.dtype)
        qp_ref[hh] = q_pe[:, hh * r:(hh + 1) * r].astype(qp_ref.dtype)


def _group_heads(w, hp):
    heads, a, b = w.shape
    return w.reshape(heads // hp, hp, a, b).transpose(0, 2, 1, 3).reshape(heads // hp, a, hp * b)


def _q_prep(cq, w_qn, w_qr, w_qr_sw, w_uk_t, cosf, sinf):
    t, ql = cq.shape
    heads, _, nope = w_qn.shape
    r = w_qr.shape[2]
    kl = w_uk_t.shape[2]
    hp = _tile(heads, (4, 2, 1))
    tm = _tile(t, (1040, 512, 256, 128, 64, 32, 16))
    hw = lambda a, b: pl.BlockSpec((1, a, b), lambda i, h: (h, 0, 0))
    rows = lambda n: pl.BlockSpec((tm, n), lambda i, h: (i, 0))
    return pl.pallas_call(
        _q_prep_kernel,
        grid=(t // tm, heads // hp),
        in_specs=[rows(ql), hw(ql, hp * nope), hw(ql, hp * r), hw(ql, hp * r),
                  pl.BlockSpec((hp, nope, kl), lambda i, h: (h, 0, 0)), rows(hp * r), rows(hp * r)],
        out_specs=[pl.BlockSpec((hp, tm, kl), lambda i, h: (h, i, 0)),
                   pl.BlockSpec((hp, tm, r), lambda i, h: (h, i, 0))],
        out_shape=[jax.ShapeDtypeStruct((heads, t, kl), BF16), jax.ShapeDtypeStruct((heads, t, r), BF16)],
        compiler_params=_cp("parallel", "arbitrary"),
        name="q_prep",
    )(cq, _group_heads(w_qn, hp), _group_heads(w_qr, hp), _group_heads(w_qr_sw, hp), w_uk_t,
      jnp.tile(cosf, (1, hp)), jnp.tile(sinf, (1, hp)))


def _o_proj_kernel(ol_ref, wuv_ref, o_in_ref, o_ref):
    del o_in_ref
    o_ref[...] = _dot(ol_ref[0], wuv_ref[0]).astype(o_ref.dtype)


def _o_proj(o_lat, w_uv_t, o_full, row_off):
    heads, n, kl = o_lat.shape
    vh = w_uv_t.shape[2]
    assert row_off % n == 0
    rb = row_off // n
    return pl.pallas_call(
        _o_proj_kernel,
        grid=(heads,),
        in_specs=[pl.BlockSpec((1, n, kl), lambda h: (h, 0, 0)), pl.BlockSpec((1, kl, vh), lambda h: (h, 0, 0)),
                  pl.BlockSpec(memory_space=pl.ANY)],
        out_specs=pl.BlockSpec((n, vh), lambda h: (rb, h)),
        out_shape=jax.ShapeDtypeStruct(o_full.shape, o_full.dtype),
        input_output_aliases={2: 0},
        compiler_params=_cp("arbitrary"),
        name="o_proj",
    )(o_lat, w_uv_t, o_full)


def _softmax_update(s, v, m_ref, l_ref, acc_ref):
    m_prev = m_ref[...]
    m_new = jnp.maximum(m_prev, jnp.max(s, axis=1, keepdims=True))
    alpha = jnp.exp(m_prev - m_new)
    p = jnp.exp(s - _rep(m_new, s.shape[1] // LANES))
    l_ref[...] = alpha * l_ref[...] + jnp.sum(p, axis=1, keepdims=True)
    acc_ref[...] = acc_ref[...] * _rep(alpha, acc_ref.shape[1] // LANES) + _dot(p.astype(BF16), v)
    m_ref[...] = m_new


def _attn_prompt_kernel(ql_ref, qp_ref, k_ref, kp_ref, wuv_ref, o_ref, m_ref, l_ref, acc_ref, *, scale):
    qi = pl.program_id(1)
    hg, tq, c = ql_ref.shape
    r = hg * tq
    q = ql_ref[...].reshape(r, c)
    qp = qp_ref[...].reshape(r, qp_ref.shape[2])
    m_ref[...] = jnp.full(m_ref.shape, NEG_INF, F32)
    l_ref[...] = jnp.zeros(l_ref.shape, F32)
    acc_ref[...] = jnp.zeros(acc_ref.shape, F32)

    def block(start, width, masked):
        k = k_ref[pl.ds(start, width), :]
        kp = kp_ref[pl.ds(start, width), :]
        s = (_dot_nt(q, k) + _dot_nt(qp, kp)) * scale
        if masked:
            s3 = s.reshape(hg, tq, width)
            qpos = qi * tq + lax.broadcasted_iota(jnp.int32, s3.shape, 1)
            kpos = start + lax.broadcasted_iota(jnp.int32, s3.shape, 2)
            s = jnp.where(kpos <= qpos, s3, NEG_INF).reshape(r, width)
        _softmax_update(s, k, m_ref, l_ref, acc_ref)

    def body(kb, carry):
        block(pl.multiple_of(kb * (2 * tq), 2 * tq), 2 * tq, False)
        return carry

    lax.fori_loop(0, qi // 2, body, 0)

    @pl.when(qi % 2 == 0)
    def _():
        block(pl.multiple_of(qi * tq, tq), tq, True)

    @pl.when(qi % 2 == 1)
    def _():
        block(pl.multiple_of((qi - 1) * tq, 2 * tq), 2 * tq, True)

    o = (acc_ref[...] / _rep(l_ref[...], c // LANES)).astype(BF16)
    vh = wuv_ref.shape[2]
    for hh in range(hg):
        o_ref[:, hh * vh:(hh + 1) * vh] = _dot(o[hh * tq:(hh + 1) * tq, :], wuv_ref[hh]).astype(o_ref.dtype)


def _attn_prompt(q_lat, q_pe, ckv_b, kpe_b, w_uv_t, batch, seq, scale):
    heads, n_tok, c = q_lat.shape
    r = q_pe.shape[2]
    vh = w_uv_t.shape[2]
    tq = _tile(seq, (256, 128))
    hg = _tile(heads, (12, 8, 4, 2, 1))
    nq = seq // tq
    rows = hg * tq
    return pl.pallas_call(
        functools.partial(_attn_prompt_kernel, scale=scale),
        grid=(batch, nq, heads // hg),
        in_specs=[pl.BlockSpec((hg, tq, c), lambda b, i, g: (g, b * nq + i, 0)),
                  pl.BlockSpec((hg, tq, r), lambda b, i, g: (g, b * nq + i, 0)),
                  pl.BlockSpec((seq, c), lambda b, i, g: (b, 0)),
                  pl.BlockSpec((seq, r), lambda b, i, g: (b, 0)),
                  pl.BlockSpec((hg, c, vh), lambda b, i, g: (g, 0, 0))],
        out_specs=pl.BlockSpec((tq, hg * vh), lambda b, i, g: (b * nq + i, g)),
        out_shape=jax.ShapeDtypeStruct((n_tok, heads * vh), BF16),
        scratch_shapes=[pltpu.VMEM((rows, LANES), F32), pltpu.VMEM((rows, LANES), F32),
                        pltpu.VMEM((rows, c), F32)],
        compiler_params=_cp("parallel", "parallel", "arbitrary"),
        name="attn_prompt",
    )(q_lat, q_pe, ckv_b, kpe_b, w_uv_t)


def _attn_sample_kernel(pt_ref, ql_ref, qp_ref, cn_ref, pn_ref, *rest, pps, scale):
    del pt_ref
    ck_refs, kp_refs = rest[:pps], rest[pps:2 * pps]
    o_ref, m_ref, l_ref, acc_ref = rest[2 * pps:]
    step = pl.program_id(1)
    q = ql_ref[0]
    qp = qp_ref[0]

    @pl.when(step == 0)
    def _():
        cn = cn_ref[0]
        pn = pn_ref[0]
        s0 = (jnp.sum(q.astype(F32) * cn, axis=1, keepdims=True)
              + jnp.sum(qp.astype(F32) * pn, axis=1, keepdims=True)) * scale
        m_ref[...] = jnp.broadcast_to(s0, m_ref.shape)
        l_ref[...] = jnp.ones(l_ref.shape, F32)
        acc_ref[...] = jnp.broadcast_to(cn, acc_ref.shape)

    k = jnp.concatenate([ref[0].astype(BF16) for ref in ck_refs], axis=0)
    kp_t = jnp.concatenate([ref[0].astype(BF16) for ref in kp_refs], axis=1)
    s = (_dot_nt(q, k) + _dot(qp, kp_t)) * scale
    _softmax_update(s, k, m_ref, l_ref, acc_ref)

    @pl.when(step == pl.num_programs(1) - 1)
    def _():
        o_ref[0] = (acc_ref[...] / _rep(l_ref[...], acc_ref.shape[1] // LANES)).astype(o_ref.dtype)


def _attn_sample(q_lat, q_pe, ckv_new, kpe_new, cache_ckv, cache_kpe_t, page_table, scale):
    bd, heads, c = q_lat.shape
    r = q_pe.shape[2]
    n_pages = page_table.shape[1]
    page = cache_ckv.shape[1]
    pps = _tile(n_pages, (32, 16, 8, 4, 2, 1))

    def page_map(i, b, s, pt):
        return (pt[b * n_pages + s * pps + i], 0, 0)

    per_seq = lambda n, w: pl.BlockSpec((1, n, w), lambda b, s, pt: (b, 0, 0))
    in_specs = [per_seq(heads, c), per_seq(heads, r), per_seq(1, c), per_seq(1, r)]
    in_specs += [pl.BlockSpec((1, page, c), functools.partial(page_map, i)) for i in range(pps)]
    in_specs += [pl.BlockSpec((1, r, page), functools.partial(page_map, i)) for i in range(pps)]
    return pl.pallas_call(
        functools.partial(_attn_sample_kernel, pps=pps, scale=scale),
        grid_spec=pltpu.PrefetchScalarGridSpec(
            num_scalar_prefetch=1,
            grid=(bd, n_pages // pps),
            in_specs=in_specs,
            out_specs=per_seq(heads, c),
            scratch_shapes=[pltpu.VMEM((heads, LANES), F32), pltpu.VMEM((heads, LANES), F32),
                            pltpu.VMEM((heads, c), F32)],
        ),
        out_shape=jax.ShapeDtypeStruct((bd, heads, c), BF16),
        compiler_params=_cp("parallel", "arbitrary"),
        name="attn_sample",
    )(page_table.reshape(-1), q_lat, q_pe, ckv_new, kpe_new, *([cache_ckv] * pps), *([cache_kpe_t] * pps))


def _s5_disc_kernel(lr_ref, li_ref, ldt_ref, br_ref, bi_ref, bbr_ref, bbi_ref, ar_ref, ai_ref):
    dt = jnp.exp(ldt_ref[...])
    lr = lr_ref[...]
    li = li_ref[...]
    mag = jnp.exp(lr * dt)
    a_re = mag * jnp.cos(li * dt)
    a_im = mag * jnp.sin(li * dt)
    den = lr * lr + li * li
    nr, ni = a_re - 1.0, a_im
    f_re = (nr * lr + ni * li) / den
    f_im = (ni * lr - nr * li) / den
    for k in range(br_ref.shape[0]):
        bbr_ref[k] = f_re * br_ref[k] - f_im * bi_ref[k]
        bbi_ref[k] = f_re * bi_ref[k] + f_im * br_ref[k]
    ar_ref[...] = a_re
    ai_ref[...] = a_im


def _s5_disc(lam_re, lam_im, log_dt, b_re_t, b_im_t):
    k, g, n = b_re_t.shape
    spec2 = pl.BlockSpec((g, n), lambda i: (0, 0))
    spec3 = lambda a: pl.BlockSpec((a, g, n), lambda i: (0, 0, 0))
    return pl.pallas_call(
        _s5_disc_kernel,
        grid=(1,),
        in_specs=[spec2, spec2, spec2, spec3(k), spec3(k)],
        out_specs=[spec3(k), spec3(k), spec2, spec2],
        out_shape=[jax.ShapeDtypeStruct((k, g, n), F32)] * 2 + [jax.ShapeDtypeStruct((g, n), F32)] * 2,
        compiler_params=_cp("arbitrary"),
        name="s5_disc",
    )(lam_re, lam_im, jnp.broadcast_to(log_dt[:, None], (g, n)), b_re_t, b_im_t)


def _idiv(x, d):
    return x >> (d.bit_length() - 1) if d & (d - 1) == 0 else x // d


def _expand_block_diag(dd, grp, n_state, cw_st):
    tiled = _rep(dd, cw_st // LANES)
    r = lax.broadcasted_iota(jnp.int32, tiled.shape, 0)
    c = lax.broadcasted_iota(jnp.int32, tiled.shape, 1)
    return jnp.where(_idiv(r, grp) == _idiv(c, n_state), tiled, jnp.zeros_like(tiled))


def _s5_scan_kernel(u_ref, bdr_ref, bdi_ref, pr_ref, pi_ref, cdr_ref, cdi_ref, d_ref, h0r_ref, h0i_ref,
                    z_ref, zb_ref, hr_out_ref, hi_out_ref,
                    bbr_ref, bbi_ref, cmr_ref, cmi_ref,
                    st_ref, up_ref, sr_ref, si_ref, ir_ref, ii_ref, cr_carry, ci_carry, *, grp, n_state):
    tc = pl.program_id(2)
    tt, w = sr_ref.shape
    nl = tt // SUBLANES
    assert nl & (nl - 1) == 0, "sub-block length must be a power of two (A_bar^nl by squaring)"

    @pl.when(tc == 0)
    def _():
        cr_carry[...] = h0r_ref[0, 0]
        ci_carry[...] = h0i_ref[0, 0]
        bbr_ref[...] = _expand_block_diag(bdr_ref[...], grp, n_state, w)
        bbi_ref[...] = _expand_block_diag(bdi_ref[...], grp, n_state, w)
        cmr_ref[...] = _expand_block_diag(cdr_ref[...], grp, n_state, w)
        cmi_ref[...] = _expand_block_diag(cdi_ref[...], grp, n_state, w)

    n_lt = u_ref.shape[1] // LANES
    pitch = st_ref.shape[1] // SUBLANES
    for s in range(SUBLANES):
        for j in range(n_lt):
            st_ref[j, s * pitch:s * pitch + nl, :] = u_ref[s * nl:(s + 1) * nl, j * LANES:(j + 1) * LANES]
    for i in range(nl):
        for j in range(n_lt):
            up_ref[i * SUBLANES:(i + 1) * SUBLANES, j * LANES:(j + 1) * LANES] = (
                st_ref[j, pl.ds(i, SUBLANES, stride=pitch), :])
    ub = up_ref[...].astype(BF16)
    sr_ref[...] = _dot(ub, bbr_ref[...])
    si_ref[...] = _dot(ub, bbi_ref[...])
    a_re = pr_ref[0]
    a_im = pi_ref[0]
    a_re8 = jnp.broadcast_to(a_re, (SUBLANES, w))
    a_im8 = jnp.broadcast_to(a_im, (SUBLANES, w))

    def pass1(i, carry):
        h_re, h_im = carry
        off = pl.multiple_of(i * SUBLANES, SUBLANES)
        x_re = sr_ref[pl.ds(off, SUBLANES), :]
        x_im = si_ref[pl.ds(off, SUBLANES), :]
        h_re, h_im = a_re8 * h_re - a_im8 * h_im + x_re, a_re8 * h_im + a_im8 * h_re + x_im
        sr_ref[pl.ds(off, SUBLANES), :] = h_re
        si_ref[pl.ds(off, SUBLANES), :] = h_im
        return h_re, h_im

    zero = jnp.zeros((SUBLANES, w), F32)
    f_re, f_im = lax.fori_loop(0, nl, pass1, (zero, zero))

    al_re, al_im = a_re, a_im
    n = nl
    while n > 1:
        al_re, al_im = al_re * al_re - al_im * al_im, 2.0 * (al_re * al_im)
        n //= 2
    c_re, c_im = cr_carry[...], ci_carry[...]
    for s in range(SUBLANES):
        ir_ref[s:s + 1, :] = c_re
        ii_ref[s:s + 1, :] = c_im
        c_re, c_im = (f_re[s:s + 1, :] + al_re * c_re - al_im * c_im,
                      f_im[s:s + 1, :] + al_re * c_im + al_im * c_re)
    cr_carry[...] = c_re
    ci_carry[...] = c_im
    i_re = ir_ref[...]
    i_im = ii_ref[...]

    def pass2(i, carry):
        p_re, p_im = carry
        off = pl.multiple_of(i * SUBLANES, SUBLANES)
        p_re8 = jnp.broadcast_to(p_re, (SUBLANES, w))
        p_im8 = jnp.broadcast_to(p_im, (SUBLANES, w))
        sr_ref[pl.ds(off, SUBLANES), :] = sr_ref[pl.ds(off, SUBLANES), :] + (p_re8 * i_re - p_im8 * i_im)
        si_ref[pl.ds(off, SUBLANES), :] = si_ref[pl.ds(off, SUBLANES), :] + (p_re8 * i_im + p_im8 * i_re)
        return p_re * a_re - p_im * a_im, p_re * a_im + p_im * a_re

    lax.fori_loop(0, nl, pass2, (a_re, a_im))

    y = _dot_nt(sr_ref[...].astype(BF16), cmr_ref[...]) - _dot_nt(si_ref[...].astype(BF16), cmi_ref[...])
    for i in range(nl):
        for j in range(n_lt):
            st_ref[j, pl.ds(i, SUBLANES, stride=pitch), :] = (
                y[i * SUBLANES:(i + 1) * SUBLANES, j * LANES:(j + 1) * LANES])
    for s in range(SUBLANES):
        rows = slice(s * nl, (s + 1) * nl)
        for j in range(n_lt):
            cols = slice(j * LANES, (j + 1) * LANES)
            z = jax.nn.gelu(st_ref[j, s * pitch:s * pitch + nl, :] + d_ref[:, cols] * u_ref[rows, cols],
                            approximate=True)
            z_ref[rows, cols] = z
            zb_ref[rows, cols] = z.astype(zb_ref.dtype)

    @pl.when(tc == pl.num_programs(2) - 1)
    def _():
        hr_out_ref[0, 0] = c_re
        hi_out_ref[0, 0] = c_im


def _s5_scan(u, row_off, batch, seq, grp, n_state, bd_re, bd_im, pw_re, pw_im, cd_re, cd_im, d_skip, h0_re, h0_im):
    n_chunks, _, cw_st = pw_re.shape
    cw_in = cw_st // n_state * grp
    tt = _tile(seq, (512, 256, 128, 64, 32, 16, 8))
    assert seq % tt == 0 and row_off % tt == 0
    nt = seq // tt
    ob = row_off // tt
    cmat = lambda a, b: pl.BlockSpec((1, a, b), lambda b_, c, t: (c, 0, 0))
    dmat = pl.BlockSpec((cw_in, LANES), lambda b_, c, t: (c, 0))
    st = pl.BlockSpec((1, 1, 1, cw_st), lambda b_, c, t: (b_, c, 0, 0))
    urow = pl.BlockSpec((tt, cw_in), lambda b_, c, t: (ob + b_ * nt + t, c))
    orow = urow
    rows = u.shape[0]
    wd = n_chunks * cw_in
    return pl.pallas_call(
        functools.partial(_s5_scan_kernel, grp=grp, n_state=n_state),
        grid=(batch, n_chunks, nt),
        in_specs=[urow, dmat, dmat, cmat(1, cw_st), cmat(1, cw_st),
                  dmat, dmat, pl.BlockSpec((1, cw_in), lambda b_, c, t: (0, c)), st, st],
        out_specs=[orow, orow, st, st],
        out_shape=[jax.ShapeDtypeStruct((rows, wd), F32), jax.ShapeDtypeStruct((rows, wd), BF16),
                   jax.ShapeDtypeStruct(h0_re.shape, F32), jax.ShapeDtypeStruct(h0_im.shape, F32)],
        scratch_shapes=[pltpu.VMEM((cw_in, cw_st), BF16), pltpu.VMEM((cw_in, cw_st), BF16),
                        pltpu.VMEM((cw_in, cw_st), BF16), pltpu.VMEM((cw_in, cw_st), BF16),
                        pltpu.VMEM((cw_in // LANES, tt + SUBLANES * SUBLANES, LANES), F32),
                        pltpu.VMEM((tt, cw_in), F32),
                        pltpu.VMEM((tt, cw_st), F32), pltpu.VMEM((tt, cw_st), F32),
                        pltpu.VMEM((SUBLANES, cw_st), F32), pltpu.VMEM((SUBLANES, cw_st), F32),
                        pltpu.VMEM((1, cw_st), F32), pltpu.VMEM((1, cw_st), F32)],
        compiler_params=_cp("parallel", "parallel", "arbitrary"),
        name="s5_scan",
    )(u, bd_re, bd_im, pw_re, pw_im, cd_re, cd_im, d_skip.reshape(1, wd), h0_re, h0_im)


def _s5_step_kernel(u_ref, bdr_ref, bdi_ref, pr_ref, pi_ref, cdr_ref, cdi_ref, d_ref, h0r_ref, h0i_ref,
                    z_in_ref, zb_in_ref, z_ref, zb_ref, hr_out_ref, hi_out_ref, *, grp, n_state):
    del z_in_ref, zb_in_ref
    w = h0r_ref.shape[1]
    u = u_ref[...]
    ub = u.astype(BF16)
    a_re = pr_ref[0]
    a_im = pi_ref[0]
    h0r = h0r_ref[...]
    h0i = h0i_ref[...]
    h_re = _dot(ub, _expand_block_diag(bdr_ref[...], grp, n_state, w)) + (a_re * h0r - a_im * h0i)
    h_im = _dot(ub, _expand_block_diag(bdi_ref[...], grp, n_state, w)) + (a_re * h0i + a_im * h0r)
    hr_out_ref[...] = h_re
    hi_out_ref[...] = h_im
    y = (_dot_nt(h_re.astype(BF16), _expand_block_diag(cdr_ref[...], grp, n_state, w))
         - _dot_nt(h_im.astype(BF16), _expand_block_diag(cdi_ref[...], grp, n_state, w)) + d_ref[...] * u)
    z = jax.nn.gelu(y, approximate=True)
    z_ref[...] = z
    zb_ref[...] = z.astype(zb_ref.dtype)


def _s5_step(u, row_off, n_seq, grp, n_state, bd_re, bd_im, pw_re, pw_im, cd_re, cd_im, d_skip, h0_re, h0_im,
             z, zb):
    n_chunks, _, cw_st = pw_re.shape
    cw_in = cw_st // n_state * grp
    ob = row_off // n_seq
    cmat = lambda a, b: pl.BlockSpec((1, a, b), lambda c: (c, 0, 0))
    dmat = pl.BlockSpec((cw_in, LANES), lambda c: (c, 0))
    st = pl.BlockSpec((n_seq, cw_st), lambda c: (0, c))
    urow = pl.BlockSpec((n_seq, cw_in), lambda c: (ob, c))
    wd = n_chunks * cw_in
    anyspec = pl.BlockSpec(memory_space=pl.ANY)
    return pl.pallas_call(
        functools.partial(_s5_step_kernel, grp=grp, n_state=n_state),
        grid=(n_chunks,),
        in_specs=[urow, dmat, dmat, cmat(1, cw_st), cmat(1, cw_st), dmat, dmat,
                  pl.BlockSpec((1, cw_in), lambda c: (0, c)), st, st, anyspec, anyspec],
        out_specs=[urow, urow, st, st],
        out_shape=[jax.ShapeDtypeStruct(z.shape, z.dtype), jax.ShapeDtypeStruct(zb.shape, zb.dtype),
                   jax.ShapeDtypeStruct(h0_re.shape, F32), jax.ShapeDtypeStruct(h0_im.shape, F32)],
        input_output_aliases={10: 0, 11: 1},
        compiler_params=_cp("parallel"),
        name="s5_step",
    )(u, bd_re, bd_im, pw_re, pw_im, cd_re, cd_im, d_skip.reshape(1, wd), h0_re, h0_im, z, zb)


def _lane_repeated(m):
    g, grp, n_state = m.shape
    assert LANES % n_state == 0
    return jnp.tile(m.reshape(g * grp, n_state), (1, LANES // n_state)).astype(BF16)


def kernel(x_prompt, x_sample, cache_ckv, cache_kpe, page_table, state_pool, state_ssm_re, state_ssm_im, norm_gains, w_ffn_gate, w_ffn_up, w_ffn_down, w_in0, w_pool, pool_scale, q_norm, kv_norm, w_uq, w_uk, w_uv, w_out0, w_in1, lam_re, lam_im, log_dt, b_re, b_im, c_re, c_im, d_skip, w_glu, w_out1):
    batch, seq, d_model = x_prompt.shape
    bd, dec_seq, _ = x_sample.shape
    assert dec_seq == 1, "the sample group carries one new token per sequence"
    n_prompt = batch * seq
    n_tok = n_prompt + bd
    pool_buf, pool_width = state_pool.shape[1], state_pool.shape[2]
    assert seq >= pool_buf and n_prompt % bd == 0
    kv_lora, heads, qk_nope = w_uk.shape
    v_head = w_uv.shape[2]
    q_lora = q_norm.shape[0]
    qk_rope = cache_kpe.shape[2]
    half = qk_rope // 2
    past_len = page_table.shape[1] * cache_ckv.shape[1]
    sm_scale = float(qk_nope + qk_rope) ** -0.5
    groups, n_state, grp = b_re.shape
    depth = norm_gains.shape[0]
    assert depth == 2

    wg, wu, wd = w_ffn_gate, w_ffn_up, w_ffn_down
    o_q, o_kv, o_pe = pool_width, pool_width + q_lora, pool_width + q_lora + kv_lora
    swap = jnp.concatenate([jnp.arange(half, qk_rope), jnp.arange(half)])
    w_kpe = w_in0[:, o_pe:].astype(BF16)
    w_kpe_sw = w_kpe[:, swap]
    w_uq3 = w_uq.astype(BF16).reshape(q_lora, heads, qk_nope + qk_rope).transpose(1, 0, 2)
    w_qn, w_qr = w_uq3[:, :, :qk_nope], w_uq3[:, :, qk_nope:]
    w_qr_sw = w_qr[:, :, swap]
    w_uk_t = w_uk.astype(BF16).transpose(1, 2, 0)
    w_uv_t = w_uv.astype(BF16).transpose(1, 0, 2)
    w_poolb = w_pool.astype(BF16)
    cache_kpe_t = cache_kpe.transpose(0, 2, 1)

    inv_freq = ROPE_THETA ** (-jnp.arange(half, dtype=F32) / half)
    pos = jnp.concatenate([jnp.tile(jnp.arange(seq), batch), jnp.full((bd,), past_len)]).astype(F32)
    ang = pos[:, None] * inv_freq[None, :]
    cos, sin = jnp.cos(ang), jnp.sin(ang)
    cosf = jnp.concatenate([cos, cos], axis=1)
    sinf = jnp.concatenate([-sin, sin], axis=1)

    g = norm_gains[0]
    x, h = _prenorm(x_prompt.reshape(n_prompt, d_model), x_sample.reshape(bd, d_model), g[0])
    x, h = _post(x, _ffn(h, wg, wu, wd, 0, 0), g[1], 0.5, g[2])

    u_pool, cq, ckv, ckv_b, kpe, kpe_b = _mla_prep(h, w_in0, o_q, o_kv, w_kpe, w_kpe_sw, q_norm, kv_norm,
                                                   cosf, sinf)
    y_pool_p = _pool_prompt(u_pool, w_poolb, pool_scale, batch, seq)
    ext_t = jnp.concatenate([state_pool.transpose(1, 0, 2), u_pool[None, n_prompt:]], axis=0)
    y_pool = _pool_sample(ext_t, w_poolb, pool_scale, past_len, y_pool_p, n_prompt)

    q_lat, q_pe = _q_prep(cq, w_qn, w_qr, w_qr_sw, w_uk_t, cosf, sinf)
    o = _attn_prompt(q_lat, q_pe, ckv_b, kpe_b, w_uv_t, batch, seq, sm_scale)
    o_lat_s = _attn_sample(q_lat[:, n_prompt:].transpose(1, 0, 2), q_pe[:, n_prompt:].transpose(1, 0, 2),
                           ckv[n_prompt:, None, :], kpe[n_prompt:, None, :],
                           cache_ckv, cache_kpe_t, page_table, sm_scale)
    o = _o_proj(o_lat_s.transpose(1, 0, 2), w_uv_t, o, n_prompt)
    y = _mm2(y_pool, o, w_out0)

    x, h = _post(x, y, g[3], 1.0, g[4])
    g1 = norm_gains[1]
    x, h = _post(x, _ffn(h, wg, wu, wd, 0, 1), g[5], 0.5, g1[0])

    x, h = _post(x, _ffn(h, wg, wu, wd, 1, 0), g1[1], 0.5, g1[2])

    u = _mm(h, w_in1)
    gpc = _tile(groups, (S5_CHUNK_GROUPS, 8, 4, 2, 1))
    n_chunks = groups // gpc
    cw_st = gpc * n_state
    bbt_re, bbt_im, pw_re, pw_im = _s5_disc(lam_re, lam_im, log_dt, b_re.transpose(2, 0, 1), b_im.transpose(2, 0, 1))
    bd_re = _lane_repeated(bbt_re.transpose(1, 0, 2))
    bd_im = _lane_repeated(bbt_im.transpose(1, 0, 2))
    cd_re = _lane_repeated(c_re)
    cd_im = _lane_repeated(c_im)
    pw_re = pw_re.reshape(n_chunks, 1, cw_st)
    pw_im = pw_im.reshape(n_chunks, 1, cw_st)
    zero_h = jnp.zeros((batch, n_chunks, 1, cw_st), F32)
    z, zb, hr_p, hi_p = _s5_scan(u, 0, batch, seq, grp, n_state, bd_re, bd_im, pw_re, pw_im, cd_re, cd_im,
                                 d_skip, zero_h, zero_h)
    z, zb, hr_s, hi_s = _s5_step(u, n_prompt, bd, grp, n_state, bd_re, bd_im, pw_re, pw_im, cd_re, cd_im,
                                 d_skip, state_ssm_re.reshape(bd, groups * n_state),
                                 state_ssm_im.reshape(bd, groups * n_state), z, zb)
    y = _mm(_mm_glu(zb, w_glu, z), w_out1)

    x, h = _post(x, y, g1[3], 1.0, g1[4])
    xp, xs = _post_last(x, _ffn(h, wg, wu, wd, 1, 1), g1[5], 0.5, n_prompt)

    y_prompt = xp.reshape(batch, seq, d_model)
    y_sample = xs.reshape(bd, 1, d_model)
    pool_p = u_pool[:n_prompt].reshape(batch, seq, pool_width)[:, seq - pool_buf:]
    pool_s = jnp.concatenate([state_pool[:, 1:], u_pool[n_prompt:, None, :]], axis=1)
    ckv_p = ckv[:n_prompt].reshape(batch, seq, kv_lora)
    ckv_s = ckv[n_prompt:].reshape(bd, 1, kv_lora)
    kpe_p = kpe[:n_prompt].reshape(batch, seq, qk_rope)
    kpe_s = kpe[n_prompt:].reshape(bd, 1, qk_rope)
    return (y_prompt, y_sample, pool_p, pool_s, ckv_p, ckv_s, kpe_p, kpe_s,
            hr_p.reshape(batch, groups, n_state), hi_p.reshape(batch, groups, n_state),
            hr_s.reshape(bd, groups, n_state), hi_s.reshape(bd, groups, n_state))
```

```python
import functools

import jax
import jax.numpy as jnp
from jax import lax
from jax.experimental import pallas as pl
from jax.experimental.pallas import tpu as pltpu

F32 = jnp.float32
BF16 = jnp.bfloat16

EPS = 1e-6
ROPE_THETA = 10000.0
POOL_WINDOWS = (2, 4, 8, 16)
NEG_INF = -1e30

V7X_VMEM_LIMIT_BYTES = 56 * 1024 * 1024
FFN_VMEM_LIMIT_BYTES = 58 * 1024 * 1024
LANES = 128
SUBLANES = 8
S5_CHUNK_GROUPS = 16


def _tile(n, prefs):
    for p in prefs:
        if n % p == 0:
            return p
    return n


def _cp(*sem):
    return pltpu.CompilerParams(dimension_semantics=sem, vmem_limit_bytes=V7X_VMEM_LIMIT_BYTES)


def _rms(x, g):
    return x * lax.rsqrt(jnp.mean(x * x, axis=-1, keepdims=True) + EPS) * g


def _dot(a, b):
    return jnp.dot(a, b, preferred_element_type=F32)


def _dot_nt(a, b):
    return lax.dot_general(a, b, (((1,), (1,)), ((), ())), preferred_element_type=F32)


def _rep(x, n):
    return x if n == 1 else jnp.concatenate([x] * n, axis=1)


def _prenorm_kernel(xp_ref, xs_ref, g_ref, x_ref, h_ref, *, n_prompt_tiles):
    i = pl.program_id(0)

    def emit(src_ref):
        x = src_ref[...]
        x_ref[...] = x
        h_ref[...] = _rms(x, g_ref[...]).astype(h_ref.dtype)

    @pl.when(i < n_prompt_tiles)
    def _():
        emit(xp_ref)

    @pl.when(i >= n_prompt_tiles)
    def _():
        emit(xs_ref)


def _prenorm(x_prompt, x_sample, g):
    n_prompt, d = x_prompt.shape
    tm = x_sample.shape[0]
    assert n_prompt % tm == 0 and tm % (2 * SUBLANES) == 0
    npt = n_prompt // tm
    t = n_prompt + tm
    row = pl.BlockSpec((tm, d), lambda i: (i, 0))
    return pl.pallas_call(
        functools.partial(_prenorm_kernel, n_prompt_tiles=npt),
        grid=(npt + 1,),
        in_specs=[pl.BlockSpec((tm, d), lambda i: (jnp.minimum(i, npt - 1), 0)),
                  pl.BlockSpec((tm, d), lambda i: (0, 0)), pl.BlockSpec((1, d), lambda i: (0, 0))],
        out_specs=[row, row],
        out_shape=[jax.ShapeDtypeStruct((t, d), F32), jax.ShapeDtypeStruct((t, d), BF16)],
        compiler_params=_cp("parallel"),
        name="prenorm",
    )(x_prompt, x_sample, g.reshape(1, d))


def _post_kernel(x_ref, y_ref, gp_ref, gn_ref, xo_ref, ho_ref, *, scale):
    xn = x_ref[...] + scale * _rms(y_ref[...], gp_ref[...])
    xo_ref[...] = xn
    ho_ref[...] = _rms(xn, gn_ref[...]).astype(ho_ref.dtype)


def _post_last_kernel(x_ref, y_ref, gp_ref, xp_ref, xs_ref, *, scale, n_prompt_tiles):
    i = pl.program_id(0)
    xn = x_ref[...] + scale * _rms(y_ref[...], gp_ref[...])

    @pl.when(i < n_prompt_tiles)
    def _():
        xp_ref[...] = xn

    @pl.when(i >= n_prompt_tiles)
    def _():
        xs_ref[...] = xn


def _post_last(x, y, g_post, scale, n_prompt):
    t, d = x.shape
    tm = t - n_prompt
    assert n_prompt % tm == 0 and tm % SUBLANES == 0
    npt = n_prompt // tm
    row = pl.BlockSpec((tm, d), lambda i: (i, 0))
    vec = pl.BlockSpec((1, d), lambda i: (0, 0))
    return pl.pallas_call(
        functools.partial(_post_last_kernel, scale=scale, n_prompt_tiles=npt),
        grid=(t // tm,),
        in_specs=[row, row, vec],
        out_specs=[pl.BlockSpec((tm, d), lambda i: (jnp.minimum(i, npt - 1), 0)),
                   pl.BlockSpec((tm, d), lambda i: (0, 0))],
        out_shape=[jax.ShapeDtypeStruct((n_prompt, d), F32), jax.ShapeDtypeStruct((tm, d), F32)],
        compiler_params=_cp("arbitrary"),
        name="post_last",
    )(x, y, g_post.reshape(1, d))


def _post(x, y, g_post, scale, g_next):
    t, d = x.shape
    tm = _tile(t, (208, 128, 64, 32, 16))
    row = pl.BlockSpec((tm, d), lambda i: (i, 0))
    vec = pl.BlockSpec((1, d), lambda i: (0, 0))
    return pl.pallas_call(
        functools.partial(_post_kernel, scale=scale),
        grid=(t // tm,),
        in_specs=[row, row, vec, vec],
        out_specs=[row, row],
        out_shape=[jax.ShapeDtypeStruct((t, d), F32), jax.ShapeDtypeStruct((t, d), BF16)],
        compiler_params=_cp("parallel"),
        name="post",
    )(x, y, g_post.reshape(1, d), g_next.reshape(1, d))


def _ffn_kernel(h_ref, wg_ref, wu_ref, wd_ref, o_ref, *, n_chunks):
    j = pl.program_id(1)

    @pl.when(j == 0)
    def _():
        o_ref[...] = jnp.zeros(o_ref.shape, o_ref.dtype)

    h = h_ref[...]
    g = _dot(h, wg_ref[...].astype(BF16))
    u = _dot(h, wu_ref[...].astype(BF16))
    a = (g * jax.nn.sigmoid(g) * u).astype(BF16)
    cw = o_ref.shape[1] // n_chunks
    for c in range(n_chunks):
        o_ref[:, c * cw:(c + 1) * cw] += _dot(a, wd_ref[:, c * cw:(c + 1) * cw].astype(BF16))


def _ffn(h, wg, wu, wd, layer, half):
    t, d = h.shape
    f = wg.shape[-1]
    tm = _tile(t, (1040, 512, 256, 128, 64, 32, 16))
    tf = _tile(f, (256, 128))
    n_chunks = max(1, d // 512)
    return pl.pallas_call(
        functools.partial(_ffn_kernel, n_chunks=n_chunks),
        grid=(t // tm, f // tf),
        in_specs=[
            pl.BlockSpec((tm, d), lambda i, j: (i, 0), pipeline_mode=pl.Buffered(1)),
            pl.BlockSpec((None, None, d, tf), lambda i, j: (layer, half, 0, j)),
            pl.BlockSpec((None, None, d, tf), lambda i, j: (layer, half, 0, j)),
            pl.BlockSpec((None, None, tf, d), lambda i, j: (layer, half, j, 0)),
        ],
        out_specs=pl.BlockSpec((tm, d), lambda i, j: (i, 0), pipeline_mode=pl.Buffered(1)),
        out_shape=jax.ShapeDtypeStruct((t, d), F32),
        compiler_params=pltpu.CompilerParams(dimension_semantics=("parallel", "arbitrary"),
                                             vmem_limit_bytes=FFN_VMEM_LIMIT_BYTES),
        name="ffn",
    )(h, wg, wu, wd)


def _mm_kernel(x_ref, w_ref, o_ref):
    o_ref[...] = _dot(x_ref[...], w_ref[...].astype(BF16)).astype(o_ref.dtype)


def _mm_specs(t, k, n):
    tm = _tile(t, (1040, 512, 256, 128, 64, 32, 16))
    tn = _tile(n, (512, 256, 128))
    return tm, tn


def _mm(x, w):
    t, k = x.shape
    n = w.shape[1]
    tm, tn = _mm_specs(t, k, n)
    return pl.pallas_call(
        _mm_kernel,
        grid=(t // tm, n // tn),
        in_specs=[pl.BlockSpec((tm, k), lambda i, j: (i, 0)), pl.BlockSpec((k, tn), lambda i, j: (0, j))],
        out_specs=pl.BlockSpec((tm, tn), lambda i, j: (i, j)),
        out_shape=jax.ShapeDtypeStruct((t, n), F32),
        compiler_params=_cp("parallel", "arbitrary"),
        name="mm",
    )(x, w)


def _mm2_kernel(x1_ref, x2_ref, w_ref, o_ref):
    k1 = x1_ref.shape[1]
    o_ref[...] = (_dot(x1_ref[...], w_ref[:k1, :].astype(BF16))
                  + _dot(x2_ref[...], w_ref[k1:, :].astype(BF16)))


def _mm2(x1, x2, w):
    t, k1 = x1.shape
    k2 = x2.shape[1]
    n = w.shape[1]
    tm, tn = _mm_specs(t, k1 + k2, n)
    return pl.pallas_call(
        _mm2_kernel,
        grid=(t // tm, n // tn),
        in_specs=[pl.BlockSpec((tm, k1), lambda i, j: (i, 0)), pl.BlockSpec((tm, k2), lambda i, j: (i, 0)),
                  pl.BlockSpec((k1 + k2, tn), lambda i, j: (0, j))],
        out_specs=pl.BlockSpec((tm, tn), lambda i, j: (i, j)),
        out_shape=jax.ShapeDtypeStruct((t, n), F32),
        compiler_params=_cp("parallel", "arbitrary"),
        name="mm2",
    )(x1, x2, w)


def _mm_glu_kernel(zb_ref, w_ref, z_ref, o_ref):
    s = _dot(zb_ref[...], w_ref[...].astype(BF16))
    o_ref[...] = (z_ref[...] * jax.nn.sigmoid(s)).astype(o_ref.dtype)


def _mm_glu(zb, w, z):
    t, k = zb.shape
    n = w.shape[1]
    tm, tn = _mm_specs(t, k, n)
    return pl.pallas_call(
        _mm_glu_kernel,
        grid=(t // tm, n // tn),
        in_specs=[pl.BlockSpec((tm, k), lambda i, j: (i, 0)), pl.BlockSpec((k, tn), lambda i, j: (0, j)),
                  pl.BlockSpec((tm, tn), lambda i, j: (i, j))],
        out_specs=pl.BlockSpec((tm, tn), lambda i, j: (i, j)),
        out_shape=jax.ShapeDtypeStruct((t, n), BF16),
        compiler_params=_cp("parallel", "arbitrary"),
        name="mm_glu",
    )(zb, w, z)


def _pool_prompt_kernel(u_ref, wp_ref, ps_ref, o_ref):
    s_len = u_ref.shape[0]
    pg = wp_ref.shape[1]
    row = lax.broadcasted_iota(jnp.int32, (s_len, pg), 0)
    for gi, w in enumerate(POOL_WINDOWS):
        cols = slice(gi * pg, (gi + 1) * pg)
        u = u_ref[:, cols]
        s = u
        d = 1
        while d < w:
            s = s + jnp.where(row >= d, pltpu.roll(s, d, axis=0), 0.0)
            d *= 2
        cnt = jnp.minimum(row + 1, w).astype(F32)
        dd = (s / cnt - u).astype(BF16)
        o_ref[:, cols] = (_dot(dd, wp_ref[gi]) * ps_ref[:, cols]).astype(o_ref.dtype)


def _pool_prompt(u_pool, w_pool, pool_scale, batch, seq):
    c = u_pool.shape[1]
    return pl.pallas_call(
        _pool_prompt_kernel,
        grid=(batch,),
        in_specs=[pl.BlockSpec((seq, c), lambda b: (b, 0)),
                  pl.BlockSpec(w_pool.shape, lambda b: (0, 0, 0)),
                  pl.BlockSpec((1, c), lambda b: (0, 0))],
        out_specs=pl.BlockSpec((seq, c), lambda b: (b, 0)),
        out_shape=jax.ShapeDtypeStruct((u_pool.shape[0], c), BF16),
        compiler_params=_cp("parallel"),
        name="pool_prompt",
    )(u_pool, w_pool, pool_scale.reshape(1, c))


def _pool_sample_kernel(ext_ref, wp_ref, ps_ref, o_in_ref, o_ref, *, start):
    del o_in_ref
    nb = ext_ref.shape[0]
    pg = wp_ref.shape[1]
    for gi, w in enumerate(POOL_WINDOWS):
        cols = slice(gi * pg, (gi + 1) * pg)
        u = ext_ref[nb - 1, :, cols]
        s = u
        for i in range(1, w):
            s = s + ext_ref[nb - 1 - i, :, cols]
        dd = (s / float(min(start + 1, w)) - u).astype(BF16)
        o_ref[:, cols] = (_dot(dd, wp_ref[gi]) * ps_ref[:, cols]).astype(o_ref.dtype)


def _pool_sample(ext_t, w_pool, pool_scale, start, y_full, row_off):
    nb, bd, c = ext_t.shape
    assert row_off % bd == 0
    rb = row_off // bd
    return pl.pallas_call(
        functools.partial(_pool_sample_kernel, start=start),
        grid=(1,),
        in_specs=[pl.BlockSpec((nb, bd, c), lambda i: (0, 0, 0)),
                  pl.BlockSpec(w_pool.shape, lambda i: (0, 0, 0)),
                  pl.BlockSpec((1, c), lambda i: (0, 0)),
                  pl.BlockSpec(memory_space=pl.ANY)],
        out_specs=pl.BlockSpec((bd, c), lambda i: (rb, 0)),
        out_shape=jax.ShapeDtypeStruct(y_full.shape, y_full.dtype),
        input_output_aliases={3: 0},
        compiler_params=_cp("arbitrary"),
        name="pool_sample",
    )(ext_t, w_pool, pool_scale.reshape(1, c), y_full)


def _mla_prep_kernel(h_ref, wp_ref, wq_ref, wkv_ref, wkp_ref, wkps_ref, qn_ref, kvn_ref, cos_ref, sin_ref,
                     up_ref, cq_ref, ckv_ref, ckvb_ref, kpe_ref, kpeb_ref):
    h = h_ref[...]
    up_ref[...] = _dot(h, wp_ref[...].astype(BF16))
    cq_ref[...] = _rms(_dot(h, wq_ref[...].astype(BF16)), qn_ref[...]).astype(cq_ref.dtype)
    ckv = _rms(_dot(h, wkv_ref[...].astype(BF16)), kvn_ref[...])
    ckv_ref[...] = ckv
    ckvb_ref[...] = ckv.astype(ckvb_ref.dtype)
    kpe = _dot(h, wkp_ref[...]) * cos_ref[...] + _dot(h, wkps_ref[...]) * sin_ref[...]
    kpe_ref[...] = kpe
    kpeb_ref[...] = kpe.astype(kpeb_ref.dtype)


def _mla_prep(h, w_in0, o_q, o_kv, w_kpe, w_kpe_sw, q_norm, kv_norm, cosf, sinf):
    t, d = h.shape
    ql, kl, r = q_norm.shape[0], kv_norm.shape[0], w_kpe.shape[1]
    pw = o_q
    assert o_q % ql == 0 and o_kv % kl == 0
    tm = _tile(t, (208, 128, 64, 32, 16))
    row = lambda n: pl.BlockSpec((tm, n), lambda i: (i, 0))
    full = lambda a, b: pl.BlockSpec((a, b), lambda i: (0, 0))
    win = lambda n, off: pl.BlockSpec((d, n), lambda i: (0, off // n), pipeline_mode=pl.Buffered(1))
    return pl.pallas_call(
        _mla_prep_kernel,
        grid=(t // tm,),
        in_specs=[row(d), win(pw, 0), win(ql, o_q), win(kl, o_kv), full(d, r), full(d, r), full(1, ql), full(1, kl),
                  row(r), row(r)],
        out_specs=[row(pw), row(ql), row(kl), row(kl), row(r), row(r)],
        out_shape=[jax.ShapeDtypeStruct((t, pw), F32), jax.ShapeDtypeStruct((t, ql), BF16),
                   jax.ShapeDtypeStruct((t, kl), F32), jax.ShapeDtypeStruct((t, kl), BF16),
                   jax.ShapeDtypeStruct((t, r), F32), jax.ShapeDtypeStruct((t, r), BF16)],
        compiler_params=_cp("parallel"),
        name="mla_prep",
    )(h, w_in0, w_in0, w_in0, w_kpe, w_kpe_sw, q_norm.reshape(1, ql), kv_norm.reshape(1, kl), cosf, sinf)


def _q_prep_kernel(cq_ref, wqn_ref, wqr_ref, wqrs_ref, wuk_ref, cos_ref, sin_ref, ql_ref, qp_ref):
    cq = cq_ref[...]
    hp, nope, _ = wuk_ref.shape
    r = qp_ref.shape[2]
    q_nope = _dot(cq, wqn_ref[0]).astype(BF16)
    q_pe = _dot(cq, wqr_ref[0]) * cos_ref[...] + _dot(cq, wqrs_ref[0]) * sin_ref[...]
    for hh in range(hp):
        ql_ref[hh] = _dot(q_nope[:, hh * nope:(hh + 1) * nope], wuk_ref[hh]).astype(ql_ref.dtype)
        qp_ref[hh] = q_pe[:, hh * r:(hh + 1) * r].astype(qp_ref.dtype)


def _group_heads(w, hp):
    heads, a, b = w.shape
    return w.reshape(heads // hp, hp, a, b).transpose(0, 2, 1, 3).reshape(heads // hp, a, hp * b)


def _q_prep(cq, w_qn, w_qr, w_qr_sw, w_uk_t, cosf, sinf):
    t, ql = cq.shape
    heads, _, nope = w_qn.shape
    r = w_qr.shape[2]
    kl = w_uk_t.shape[2]
    hp = _tile(heads, (4, 2, 1))
    tm = _tile(t, (1040, 512, 256, 128, 64, 32, 16))
    hw = lambda a, b: pl.BlockSpec((1, a, b), lambda i, h: (h, 0, 0))
    rows = lambda n: pl.BlockSpec((tm, n), lambda i, h: (i, 0))
    return pl.pallas_call(
        _q_prep_kernel,
        grid=(t // tm, heads // hp),
        in_specs=[rows(ql), hw(ql, hp * nope), hw(ql, hp * r), hw(ql, hp * r),
                  pl.BlockSpec((hp, nope, kl), lambda i, h: (h, 0, 0)), rows(hp * r), rows(hp * r)],
        out_specs=[pl.BlockSpec((hp, tm, kl), lambda i, h: (h, i, 0)),
                   pl.BlockSpec((hp, tm, r), lambda i, h: (h, i, 0))],
        out_shape=[jax.ShapeDtypeStruct((heads, t, kl), BF16), jax.ShapeDtypeStruct((heads, t, r), BF16)],
        compiler_params=_cp("parallel", "arbitrary"),
        name="q_prep",
    )(cq, _group_heads(w_qn, hp), _group_heads(w_qr, hp), _group_heads(w_qr_sw, hp), w_uk_t,
      jnp.tile(cosf, (1, hp)), jnp.tile(sinf, (1, hp)))


def _o_proj_kernel(ol_ref, wuv_ref, o_in_ref, o_ref):
    del o_in_ref
    o_ref[...] = _dot(ol_ref[0], wuv_ref[0]).astype(o_ref.dtype)


def _o_proj(o_lat, w_uv_t, o_full, row_off):
    heads, n, kl = o_lat.shape
    vh = w_uv_t.shape[2]
    assert row_off % n == 0
    rb = row_off // n
    return pl.pallas_call(
        _o_proj_kernel,
        grid=(heads,),
        in_specs=[pl.BlockSpec((1, n, kl), lambda h: (h, 0, 0)), pl.BlockSpec((1, kl, vh), lambda h: (h, 0, 0)),
                  pl.BlockSpec(memory_space=pl.ANY)],
        out_specs=pl.BlockSpec((n, vh), lambda h: (rb, h)),
        out_shape=jax.ShapeDtypeStruct(o_full.shape, o_full.dtype),
        input_output_aliases={2: 0},
        compiler_params=_cp("arbitrary"),
        name="o_proj",
    )(o_lat, w_uv_t, o_full)


def _softmax_update(s, v, m_ref, l_ref, acc_ref):
    m_prev = m_ref[...]
    m_new = jnp.maximum(m_prev, jnp.max(s, axis=1, keepdims=True))
    alpha = jnp.exp(m_prev - m_new)
    p = jnp.exp(s - _rep(m_new, s.shape[1] // LANES))
    l_ref[...] = alpha * l_ref[...] + jnp.sum(p, axis=1, keepdims=True)
    acc_ref[...] = acc_ref[...] * _rep(alpha, acc_ref.shape[1] // LANES) + _dot(p.astype(BF16), v)
    m_ref[...] = m_new


def _attn_prompt_kernel(ql_ref, qp_ref, k_ref, kp_ref, wuv_ref, o_ref, m_ref, l_ref, acc_ref, *, scale):
    qi = pl.program_id(1)
    hg, tq, c = ql_ref.shape
    r = hg * tq
    q = ql_ref[...].reshape(r, c)
    qp = qp_ref[...].reshape(r, qp_ref.shape[2])
    m_ref[...] = jnp.full(m_ref.shape, NEG_INF, F32)
    l_ref[...] = jnp.zeros(l_ref.shape, F32)
    acc_ref[...] = jnp.zeros(acc_ref.shape, F32)

    def block(start, width, masked):
        k = k_ref[pl.ds(start, width), :]
        kp = kp_ref[pl.ds(start, width), :]
        s = (_dot_nt(q, k) + _dot_nt(qp, kp)) * scale
        if masked:
            s3 = s.reshape(hg, tq, width)
            qpos = qi * tq + lax.broadcasted_iota(jnp.int32, s3.shape, 1)
            kpos = start + lax.broadcasted_iota(jnp.int32, s3.shape, 2)
            s = jnp.where(kpos <= qpos, s3, NEG_INF).reshape(r, width)
        _softmax_update(s, k, m_ref, l_ref, acc_ref)

    def body(kb, carry):
        block(pl.multiple_of(kb * (2 * tq), 2 * tq), 2 * tq, False)
        return carry

    lax.fori_loop(0, qi // 2, body, 0)

    @pl.when(qi % 2 == 0)
    def _():
        block(pl.multiple_of(qi * tq, tq), tq, True)

    @pl.when(qi % 2 == 1)
    def _():
        block(pl.multiple_of((qi - 1) * tq, 2 * tq), 2 * tq, True)

    o = (acc_ref[...] / _rep(l_ref[...], c // LANES)).astype(BF16)
    vh = wuv_ref.shape[2]
    for hh in range(hg):
        o_ref[:, hh * vh:(hh + 1) * vh] = _dot(o[hh * tq:(hh + 1) * tq, :], wuv_ref[hh]).astype(o_ref.dtype)


def _attn_prompt(q_lat, q_pe, ckv_b, kpe_b, w_uv_t, batch, seq, scale):
    heads, n_tok, c = q_lat.shape
    r = q_pe.shape[2]
    vh = w_uv_t.shape[2]
    tq = _tile(seq, (256, 128))
    hg = _tile(heads, (12, 8, 4, 2, 1))
    nq = seq // tq
    rows = hg * tq
    return pl.pallas_call(
        functools.partial(_attn_prompt_kernel, scale=scale),
        grid=(batch, nq, heads // hg),
        in_specs=[pl.BlockSpec((hg, tq, c), lambda b, i, g: (g, b * nq + i, 0)),
                  pl.BlockSpec((hg, tq, r), lambda b, i, g: (g, b * nq + i, 0)),
                  pl.BlockSpec((seq, c), lambda b, i, g: (b, 0)),
                  pl.BlockSpec((seq, r), lambda b, i, g: (b, 0)),
                  pl.BlockSpec((hg, c, vh), lambda b, i, g: (g, 0, 0))],
        out_specs=pl.BlockSpec((tq, hg * vh), lambda b, i, g: (b * nq + i, g)),
        out_shape=jax.ShapeDtypeStruct((n_tok, heads * vh), BF16),
        scratch_shapes=[pltpu.VMEM((rows, LANES), F32), pltpu.VMEM((rows, LANES), F32),
                        pltpu.VMEM((rows, c), F32)],
        compiler_params=_cp("parallel", "parallel", "arbitrary"),
        name="attn_prompt",
    )(q_lat, q_pe, ckv_b, kpe_b, w_uv_t)


def _attn_sample_kernel(pt_ref, ql_ref, qp_ref, cn_ref, pn_ref, ck_hbm, kp_hbm, o_ref,
                        kbuf, pbuf, sem, m_ref, l_ref, acc_ref, *, pps, scale):
    n_steps = pl.num_programs(1)
    step = pl.program_id(1)
    gstep = pl.program_id(0) * n_steps + step
    total = pl.num_programs(0) * n_steps
    slot = gstep % 2

    def page_copies(g, slot_idx, i):
        pid = pt_ref[g * pps + i]
        return (pltpu.make_async_copy(ck_hbm.at[pid], kbuf.at[slot_idx, i], sem.at[0, slot_idx]),
                pltpu.make_async_copy(kp_hbm.at[pid], pbuf.at[slot_idx, i], sem.at[1, slot_idx]))

    def start_fetch(g, slot_idx):
        for i in range(pps):
            for cp in page_copies(g, slot_idx, i):
                cp.start()

    @pl.when(gstep == 0)
    def _():
        start_fetch(0, 0)

    @pl.when(gstep + 1 < total)
    def _():
        start_fetch(gstep + 1, 1 - slot)

    for i in range(pps):
        for cp in page_copies(gstep, slot, i):
            cp.wait()

    q = ql_ref[0]
    qp = qp_ref[0]

    @pl.when(step == 0)
    def _():
        cn = cn_ref[0]
        pn = pn_ref[0]
        s0 = (jnp.sum(q.astype(F32) * cn, axis=1, keepdims=True)
              + jnp.sum(qp.astype(F32) * pn, axis=1, keepdims=True)) * scale
        m_ref[...] = jnp.broadcast_to(s0, m_ref.shape)
        l_ref[...] = jnp.ones(l_ref.shape, F32)
        acc_ref[...] = jnp.broadcast_to(cn, acc_ref.shape)

    page, c = kbuf.shape[2], kbuf.shape[3]
    k = kbuf[slot].reshape(pps * page, c).astype(BF16)
    kp_t = jnp.concatenate([pbuf[slot, i].astype(BF16) for i in range(pps)], axis=1)
    s = (_dot_nt(q, k) + _dot(qp, kp_t)) * scale
    _softmax_update(s, k, m_ref, l_ref, acc_ref)

    @pl.when(step == pl.num_programs(1) - 1)
    def _():
        o_ref[0] = (acc_ref[...] / _rep(l_ref[...], acc_ref.shape[1] // LANES)).astype(o_ref.dtype)


def _attn_sample(q_lat, q_pe, ckv_new, kpe_new, cache_ckv, cache_kpe_t, page_table, scale):
    bd, heads, c = q_lat.shape
    r = q_pe.shape[2]
    n_pages = page_table.shape[1]
    page = cache_ckv.shape[1]
    pps = _tile(n_pages, (32, 16, 8, 4, 2, 1))

    per_seq = lambda n, w: pl.BlockSpec((1, n, w), lambda b, s, pt: (b, 0, 0))
    hbm = pl.BlockSpec(memory_space=pl.ANY)
    return pl.pallas_call(
        functools.partial(_attn_sample_kernel, pps=pps, scale=scale),
        grid_spec=pltpu.PrefetchScalarGridSpec(
            num_scalar_prefetch=1,
            grid=(bd, n_pages // pps),
            in_specs=[per_seq(heads, c), per_seq(heads, r), per_seq(1, c), per_seq(1, r), hbm, hbm],
            out_specs=per_seq(heads, c),
            scratch_shapes=[pltpu.VMEM((2, pps, page, c), F32), pltpu.VMEM((2, pps, r, page), F32),
                            pltpu.SemaphoreType.DMA((2, 2)),
                            pltpu.VMEM((heads, LANES), F32), pltpu.VMEM((heads, LANES), F32),
                            pltpu.VMEM((heads, c), F32)],
        ),
        out_shape=jax.ShapeDtypeStruct((bd, heads, c), BF16),
        compiler_params=_cp("arbitrary", "arbitrary"),
        name="attn_sample",
    )(page_table.reshape(-1), q_lat, q_pe, ckv_new, kpe_new, cache_ckv, cache_kpe_t)


def _s5_disc_kernel(lr_ref, li_ref, ldt_ref, br_ref, bi_ref, bbr_ref, bbi_ref, ar_ref, ai_ref):
    dt = jnp.exp(ldt_ref[...])
    lr = lr_ref[...]
    li = li_ref[...]
    mag = jnp.exp(lr * dt)
    a_re = mag * jnp.cos(li * dt)
    a_im = mag * jnp.sin(li * dt)
    den = lr * lr + li * li
    nr, ni = a_re - 1.0, a_im
    f_re = (nr * lr + ni * li) / den
    f_im = (ni * lr - nr * li) / den
    for k in range(br_ref.shape[0]):
        bbr_ref[k] = f_re * br_ref[k] - f_im * bi_ref[k]
        bbi_ref[k] = f_re * bi_ref[k] + f_im * br_ref[k]
    ar_ref[...] = a_re
    ai_ref[...] = a_im


def _s5_disc(lam_re, lam_im, log_dt, b_re_t, b_im_t):
    k, g, n = b_re_t.shape
    spec2 = pl.BlockSpec((g, n), lambda i: (0, 0))
    spec3 = lambda a: pl.BlockSpec((a, g, n), lambda i: (0, 0, 0))
    return pl.pallas_call(
        _s5_disc_kernel,
        grid=(1,),
        in_specs=[spec2, spec2, spec2, spec3(k), spec3(k)],
        out_specs=[spec3(k), spec3(k), spec2, spec2],
        out_shape=[jax.ShapeDtypeStruct((k, g, n), F32)] * 2 + [jax.ShapeDtypeStruct((g, n), F32)] * 2,
        compiler_params=_cp("arbitrary"),
        name="s5_disc",
    )(lam_re, lam_im, jnp.broadcast_to(log_dt[:, None], (g, n)), b_re_t, b_im_t)


def _idiv(x, d):
    return x >> (d.bit_length() - 1) if d & (d - 1) == 0 else x // d


def _expand_block_diag(dd, grp, n_state, cw_st):
    tiled = _rep(dd, cw_st // LANES)
    r = lax.broadcasted_iota(jnp.int32, tiled.shape, 0)
    c = lax.broadcasted_iota(jnp.int32, tiled.shape, 1)
    return jnp.where(_idiv(r, grp) == _idiv(c, n_state), tiled, jnp.zeros_like(tiled))


def _s5_scan_kernel(u_ref, bdr_ref, bdi_ref, pr_ref, pi_ref, cdr_ref, cdi_ref, d_ref, h0r_ref, h0i_ref,
                    z_ref, zb_ref, hr_out_ref, hi_out_ref,
                    bbr_ref, bbi_ref, cmr_ref, cmi_ref,
                    st_ref, up_ref, sr_ref, si_ref, ir_ref, ii_ref, cr_carry, ci_carry, *, grp, n_state):
    tc = pl.program_id(2)
    tt, w = sr_ref.shape
    nl = tt // SUBLANES
    assert nl & (nl - 1) == 0, "sub-block length must be a power of two (A_bar^nl by squaring)"

    @pl.when(tc == 0)
    def _():
        cr_carry[...] = h0r_ref[0, 0]
        ci_carry[...] = h0i_ref[0, 0]
        bbr_ref[...] = _expand_block_diag(bdr_ref[...], grp, n_state, w)
        bbi_ref[...] = _expand_block_diag(bdi_ref[...], grp, n_state, w)
        cmr_ref[...] = _expand_block_diag(cdr_ref[...], grp, n_state, w)
        cmi_ref[...] = _expand_block_diag(cdi_ref[...], grp, n_state, w)

    n_lt = u_ref.shape[1] // LANES
    pitch = st_ref.shape[1] // SUBLANES
    for s in range(SUBLANES):
        for j in range(n_lt):
            st_ref[j, s * pitch:s * pitch + nl, :] = u_ref[s * nl:(s + 1) * nl, j * LANES:(j + 1) * LANES]
    for i in range(nl):
        for j in range(n_lt):
            up_ref[i * SUBLANES:(i + 1) * SUBLANES, j * LANES:(j + 1) * LANES] = (
                st_ref[j, pl.ds(i, SUBLANES, stride=pitch), :])
    ub = up_ref[...].astype(BF16)
    sr_ref[...] = _dot(ub, bbr_ref[...])
    si_ref[...] = _dot(ub, bbi_ref[...])
    a_re = pr_ref[0]
    a_im = pi_ref[0]
    a_re8 = jnp.broadcast_to(a_re, (SUBLANES, w))
    a_im8 = jnp.broadcast_to(a_im, (SUBLANES, w))

    def pass1(i, carry):
        h_re, h_im = carry
        off = pl.multiple_of(i * SUBLANES, SUBLANES)
        x_re = sr_ref[pl.ds(off, SUBLANES), :]
        x_im = si_ref[pl.ds(off, SUBLANES), :]
        h_re, h_im = a_re8 * h_re - a_im8 * h_im + x_re, a_re8 * h_im + a_im8 * h_re + x_im
        sr_ref[pl.ds(off, SUBLANES), :] = h_re
        si_ref[pl.ds(off, SUBLANES), :] = h_im
        return h_re, h_im

    zero = jnp.zeros((SUBLANES, w), F32)
    f_re, f_im = lax.fori_loop(0, nl, pass1, (zero, zero))

    al_re, al_im = a_re, a_im
    n = nl
    while n > 1:
        al_re, al_im = al_re * al_re - al_im * al_im, 2.0 * (al_re * al_im)
        n //= 2
    c_re, c_im = cr_carry[...], ci_carry[...]
    for s in range(SUBLANES):
        ir_ref[s:s + 1, :] = c_re
        ii_ref[s:s + 1, :] = c_im
        c_re, c_im = (f_re[s:s + 1, :] + al_re * c_re - al_im * c_im,
                      f_im[s:s + 1, :] + al_re * c_im + al_im * c_re)
    cr_carry[...] = c_re
    ci_carry[...] = c_im
    i_re = ir_ref[...]
    i_im = ii_ref[...]

    def pass2(i, carry):
        p_re, p_im = carry
        off = pl.multiple_of(i * SUBLANES, SUBLANES)
        p_re8 = jnp.broadcast_to(p_re, (SUBLANES, w))
        p_im8 = jnp.broadcast_to(p_im, (SUBLANES, w))
        sr_ref[pl.ds(off, SUBLANES), :] = sr_ref[pl.ds(off, SUBLANES), :] + (p_re8 * i_re - p_im8 * i_im)
        si_ref[pl.ds(off, SUBLANES), :] = si_ref[pl.ds(off, SUBLANES), :] + (p_re8 * i_im + p_im8 * i_re)
        return p_re * a_re - p_im * a_im, p_re * a_im + p_im * a_re

    lax.fori_loop(0, nl, pass2, (a_re, a_im))

    y = _dot_nt(sr_ref[...].astype(BF16), cmr_ref[...]) - _dot_nt(si_ref[...].astype(BF16), cmi_ref[...])
    for i in range(nl):
        for j in range(n_lt):
            st_ref[j, pl.ds(i, SUBLANES, stride=pitch), :] = (
                y[i * SUBLANES:(i + 1) * SUBLANES, j * LANES:(j + 1) * LANES])
    for s in range(SUBLANES):
        rows = slice(s * nl, (s + 1) * nl)
        for j in range(n_lt):
            cols = slice(j * LANES, (j + 1) * LANES)
            z = jax.nn.gelu(st_ref[j, s * pitch:s * pitch + nl, :] + d_ref[:, cols] * u_ref[rows, cols],
                            approximate=True)
            z_ref[rows, cols] = z
            zb_ref[rows, cols] = z.astype(zb_ref.dtype)

    @pl.when(tc == pl.num_programs(2) - 1)
    def _():
        hr_out_ref[0, 0] = c_re
        hi_out_ref[0, 0] = c_im


def _s5_scan(u, row_off, batch, seq, grp, n_state, bd_re, bd_im, pw_re, pw_im, cd_re, cd_im, d_skip, h0_re, h0_im):
    n_chunks, _, cw_st = pw_re.shape
    cw_in = cw_st // n_state * grp
    tt = _tile(seq, (512, 256, 128, 64, 32, 16, 8))
    assert seq % tt == 0 and row_off % tt == 0
    nt = seq // tt
    ob = row_off // tt
    cmat = lambda a, b: pl.BlockSpec((1, a, b), lambda b_, c, t: (c, 0, 0))
    dmat = pl.BlockSpec((cw_in, LANES), lambda b_, c, t: (c, 0))
    st = pl.BlockSpec((1, 1, 1, cw_st), lambda b_, c, t: (b_, c, 0, 0))
    urow = pl.BlockSpec((tt, cw_in), lambda b_, c, t: (ob + b_ * nt + t, c))
    orow = urow
    rows = u.shape[0]
    wd = n_chunks * cw_in
    return pl.pallas_call(
        functools.partial(_s5_scan_kernel, grp=grp, n_state=n_state),
        grid=(batch, n_chunks, nt),
        in_specs=[urow, dmat, dmat, cmat(1, cw_st), cmat(1, cw_st),
                  dmat, dmat, pl.BlockSpec((1, cw_in), lambda b_, c, t: (0, c)), st, st],
        out_specs=[orow, orow, st, st],
        out_shape=[jax.ShapeDtypeStruct((rows, wd), F32), jax.ShapeDtypeStruct((rows, wd), BF16),
                   jax.ShapeDtypeStruct(h0_re.shape, F32), jax.ShapeDtypeStruct(h0_im.shape, F32)],
        scratch_shapes=[pltpu.VMEM((cw_in, cw_st), BF16), pltpu.VMEM((cw_in, cw_st), BF16),
                        pltpu.VMEM((cw_in, cw_st), BF16), pltpu.VMEM((cw_in, cw_st), BF16),
                        pltpu.VMEM((cw_in // LANES, tt + SUBLANES * SUBLANES, LANES), F32),
                        pltpu.VMEM((tt, cw_in), F32),
                        pltpu.VMEM((tt, cw_st), F32), pltpu.VMEM((tt, cw_st), F32),
                        pltpu.VMEM((SUBLANES, cw_st), F32), pltpu.VMEM((SUBLANES, cw_st), F32),
                        pltpu.VMEM((1, cw_st), F32), pltpu.VMEM((1, cw_st), F32)],
        compiler_params=_cp("parallel", "parallel", "arbitrary"),
        name="s5_scan",
    )(u, bd_re, bd_im, pw_re, pw_im, cd_re, cd_im, d_skip.reshape(1, wd), h0_re, h0_im)


def _s5_step_kernel(u_ref, bdr_ref, bdi_ref, pr_ref, pi_ref, cdr_ref, cdi_ref, d_ref, h0r_ref, h0i_ref,
                    z_in_ref, zb_in_ref, z_ref, zb_ref, hr_out_ref, hi_out_ref, *, grp, n_state):
    del z_in_ref, zb_in_ref
    w = h0r_ref.shape[1]
    u = u_ref[...]
    ub = u.astype(BF16)
    a_re = pr_ref[0]
    a_im = pi_ref[0]
    h0r = h0r_ref[...]
    h0i = h0i_ref[...]
    h_re = _dot(ub, _expand_block_diag(bdr_ref[...], grp, n_state, w)) + (a_re * h0r - a_im * h0i)
    h_im = _dot(ub, _expand_block_diag(bdi_ref[...], grp, n_state, w)) + (a_re * h0i + a_im * h0r)
    hr_out_ref[...] = h_re
    hi_out_ref[...] = h_im
    y = (_dot_nt(h_re.astype(BF16), _expand_block_diag(cdr_ref[...], grp, n_state, w))
         - _dot_nt(h_im.astype(BF16), _expand_block_diag(cdi_ref[...], grp, n_state, w)) + d_ref[...] * u)
    z = jax.nn.gelu(y, approximate=True)
    z_ref[...] = z
    zb_ref[...] = z.astype(zb_ref.dtype)


def _s5_step(u, row_off, n_seq, grp, n_state, bd_re, bd_im, pw_re, pw_im, cd_re, cd_im, d_skip, h0_re, h0_im,
             z, zb):
    n_chunks, _, cw_st = pw_re.shape
    cw_in = cw_st // n_state * grp
    ob = row_off // n_seq
    cmat = lambda a, b: pl.BlockSpec((1, a, b), lambda c: (c, 0, 0))
    dmat = pl.BlockSpec((cw_in, LANES), lambda c: (c, 0))
    st = pl.BlockSpec((n_seq, cw_st), lambda c: (0, c))
    urow = pl.BlockSpec((n_seq, cw_in), lambda c: (ob, c))
    wd = n_chunks * cw_in
    anyspec = pl.BlockSpec(memory_space=pl.ANY)
    return pl.pallas_call(
        functools.partial(_s5_step_kernel, grp=grp, n_state=n_state),
        grid=(n_chunks,),
        in_specs=[urow, dmat, dmat, cmat(1, cw_st), cmat(1, cw_st), dmat, dmat,
                  pl.BlockSpec((1, cw_in), lambda c: (0, c)), st, st, anyspec, anyspec],
        out_specs=[urow, urow, st, st],
        out_shape=[jax.ShapeDtypeStruct(z.shape, z.dtype), jax.ShapeDtypeStruct(zb.shape, zb.dtype),
                   jax.ShapeDtypeStruct(h0_re.shape, F32), jax.ShapeDtypeStruct(h0_im.shape, F32)],
        input_output_aliases={10: 0, 11: 1},
        compiler_params=_cp("parallel"),
        name="s5_step",
    )(u, bd_re, bd_im, pw_re, pw_im, cd_re, cd_im, d_skip.reshape(1, wd), h0_re, h0_im, z, zb)


def _lane_repeated(m):
    g, grp, n_state = m.shape
    assert LANES % n_state == 0
    return jnp.tile(m.reshape(g * grp, n_state), (1, LANES // n_state)).astype(BF16)


def kernel(x_prompt, x_sample, cache_ckv, cache_kpe, page_table, state_pool, state_ssm_re, state_ssm_im, norm_gains, w_ffn_gate, w_ffn_up, w_ffn_down, w_in0, w_pool, pool_scale, q_norm, kv_norm, w_uq, w_uk, w_uv, w_out0, w_in1, lam_re, lam_im, log_dt, b_re, b_im, c_re, c_im, d_skip, w_glu, w_out1):
    batch, seq, d_model = x_prompt.shape
    bd, dec_seq, _ = x_sample.shape
    assert dec_seq == 1, "the sample group carries one new token per sequence"
    n_prompt = batch * seq
    n_tok = n_prompt + bd
    pool_buf, pool_width = state_pool.shape[1], state_pool.shape[2]
    assert seq >= pool_buf and n_prompt % bd == 0
    kv_lora, heads, qk_nope = w_uk.shape
    v_head = w_uv.shape[2]
    q_lora = q_norm.shape[0]
    qk_rope = cache_kpe.shape[2]
    half = qk_rope // 2
    past_len = page_table.shape[1] * cache_ckv.shape[1]
    sm_scale = float(qk_nope + qk_rope) ** -0.5
    groups, n_state, grp = b_re.shape
    depth = norm_gains.shape[0]
    assert depth == 2

    wg, wu, wd = w_ffn_gate, w_ffn_up, w_ffn_down
    o_q, o_kv, o_pe = pool_width, pool_width + q_lora, pool_width + q_lora + kv_lora
    swap = jnp.concatenate([jnp.arange(half, qk_rope), jnp.arange(half)])
    w_kpe = w_in0[:, o_pe:].astype(BF16)
    w_kpe_sw = w_kpe[:, swap]
    w_uq3 = w_uq.astype(BF16).reshape(q_lora, heads, qk_nope + qk_rope).transpose(1, 0, 2)
    w_qn, w_qr = w_uq3[:, :, :qk_nope], w_uq3[:, :, qk_nope:]
    w_qr_sw = w_qr[:, :, swap]
    w_uk_t = w_uk.astype(BF16).transpose(1, 2, 0)
    w_uv_t = w_uv.astype(BF16).transpose(1, 0, 2)
    w_poolb = w_pool.astype(BF16)
    cache_kpe_t = cache_kpe.transpose(0, 2, 1)

    inv_freq = ROPE_THETA ** (-jnp.arange(half, dtype=F32) / half)
    pos = jnp.concatenate([jnp.tile(jnp.arange(seq), batch), jnp.full((bd,), past_len)]).astype(F32)
    ang = pos[:, None] * inv_freq[None, :]
    cos, sin = jnp.cos(ang), jnp.sin(ang)
    cosf = jnp.concatenate([cos, cos], axis=1)
    sinf = jnp.concatenate([-sin, sin], axis=1)

    g = norm_gains[0]
    x, h = _prenorm(x_prompt.reshape(n_prompt, d_model), x_sample.reshape(bd, d_model), g[0])
    x, h = _post(x, _ffn(h, wg, wu, wd, 0, 0), g[1], 0.5, g[2])

    u_pool, cq, ckv, ckv_b, kpe, kpe_b = _mla_prep(h, w_in0, o_q, o_kv, w_kpe, w_kpe_sw, q_norm, kv_norm,
                                                   cosf, sinf)
    y_pool_p = _pool_prompt(u_pool, w_poolb, pool_scale, batch, seq)
    ext_t = jnp.concatenate([state_pool.transpose(1, 0, 2), u_pool[None, n_prompt:]], axis=0)
    y_pool = _pool_sample(ext_t, w_poolb, pool_scale, past_len, y_pool_p, n_prompt)

    q_lat, q_pe = _q_prep(cq, w_qn, w_qr, w_qr_sw, w_uk_t, cosf, sinf)
    o = _attn_prompt(q_lat, q_pe, ckv_b, kpe_b, w_uv_t, batch, seq, sm_scale)
    o_lat_s = _attn_sample(q_lat[:, n_prompt:].transpose(1, 0, 2), q_pe[:, n_prompt:].transpose(1, 0, 2),
                           ckv[n_prompt:, None, :], kpe[n_prompt:, None, :],
                           cache_ckv, cache_kpe_t, page_table, sm_scale)
    o = _o_proj(o_lat_s.transpose(1, 0, 2), w_uv_t, o, n_prompt)
    y = _mm2(y_pool, o, w_out0)

    x, h = _post(x, y, g[3], 1.0, g[4])
    g1 = norm_gains[1]
    x, h = _post(x, _ffn(h, wg, wu, wd, 0, 1), g[5], 0.5, g1[0])

    x, h = _post(x, _ffn(h, wg, wu, wd, 1, 0), g1[1], 0.5, g1[2])

    u = _mm(h, w_in1)
    gpc = _tile(groups, (S5_CHUNK_GROUPS, 8, 4, 2, 1))
    n_chunks = groups // gpc
    cw_st = gpc * n_state
    bbt_re, bbt_im, pw_re, pw_im = _s5_disc(lam_re, lam_im, log_dt, b_re.transpose(2, 0, 1), b_im.transpose(2, 0, 1))
    bd_re = _lane_repeated(bbt_re.transpose(1, 0, 2))
    bd_im = _lane_repeated(bbt_im.transpose(1, 0, 2))
    cd_re = _lane_repeated(c_re)
    cd_im = _lane_repeated(c_im)
    pw_re = pw_re.reshape(n_chunks, 1, cw_st)
    pw_im = pw_im.reshape(n_chunks, 1, cw_st)
    zero_h = jnp.zeros((batch, n_chunks, 1, cw_st), F32)
    z, zb, hr_p, hi_p = _s5_scan(u, 0, batch, seq, grp, n_state, bd_re, bd_im, pw_re, pw_im, cd_re, cd_im,
                                 d_skip, zero_h, zero_h)
    z, zb, hr_s, hi_s = _s5_step(u, n_prompt, bd, grp, n_state, bd_re, bd_im, pw_re, pw_im, cd_re, cd_im,
                                 d_skip, state_ssm_re.reshape(bd, groups * n_state),
                                 state_ssm_im.reshape(bd, groups * n_state), z, zb)
    y = _mm(_mm_glu(zb, w_glu, z), w_out1)

    x, h = _post(x, y, g1[3], 1.0, g1[4])
    xp, xs = _post_last(x, _ffn(h, wg, wu, wd, 1, 1), g1[5], 0.5, n_prompt)

    y_prompt = xp.reshape(batch, seq, d_model)
    y_sample = xs.reshape(bd, 1, d_model)
    pool_p = u_pool[:n_prompt].reshape(batch, seq, pool_width)[:, seq - pool_buf:]
    pool_s = jnp.concatenate([state_pool[:, 1:], u_pool[n_prompt:, None, :]], axis=1)
    ckv_p = ckv[:n_prompt].reshape(batch, seq, kv_lora)
    ckv_s = ckv[n_prompt:].reshape(bd, 1, kv_lora)
    kpe_p = kpe[:n_prompt].reshape(batch, seq, qk_rope)
    kpe_s = kpe[n_prompt:].reshape(bd, 1, qk_rope)
    return (y_prompt, y_sample, pool_p, pool_s, ckv_p, ckv_s, kpe_p, kpe_s,
            hr_p.reshape(batch, groups, n_state), hi_p.reshape(batch, groups, n_state),
            hr_s.reshape(bd, groups, n_state), hi_s.reshape(bd, groups, n_state))
```

```python
import functools

import jax
import jax.numpy as jnp
from jax import lax
from jax.experimental import pallas as pl
from jax.experimental.pallas import tpu as pltpu

F32 = jnp.float32
BF16 = jnp.bfloat16

EPS = 1e-6
ROPE_THETA = 10000.0
POOL_WINDOWS = (2, 4, 8, 16)
NEG_INF = -1e30

V7X_VMEM_LIMIT_BYTES = 56 * 1024 * 1024
FFN_VMEM_LIMIT_BYTES = 58 * 1024 * 1024
LANES = 128
SUBLANES = 8
S5_CHUNK_GROUPS = 16
PAGE_RING_SLOTS = 3


def _tile(n, prefs):
    for p in prefs:
        if n % p == 0:
            return p
    return n


def _cp(*sem):
    return pltpu.CompilerParams(dimension_semantics=sem, vmem_limit_bytes=V7X_VMEM_LIMIT_BYTES)


def _rms(x, g):
    return x * lax.rsqrt(jnp.mean(x * x, axis=-1, keepdims=True) + EPS) * g


def _dot(a, b):
    return jnp.dot(a, b, preferred_element_type=F32)


def _dot_nt(a, b):
    return lax.dot_general(a, b, (((1,), (1,)), ((), ())), preferred_element_type=F32)


def _rep(x, n):
    return x if n == 1 else jnp.concatenate([x] * n, axis=1)


def _prenorm_kernel(xp_ref, xs_ref, g_ref, x_ref, h_ref, *, n_prompt_tiles):
    i = pl.program_id(0)

    def emit(src_ref):
        x = src_ref[...]
        x_ref[...] = x
        h_ref[...] = _rms(x, g_ref[...]).astype(h_ref.dtype)

    @pl.when(i < n_prompt_tiles)
    def _():
        emit(xp_ref)

    @pl.when(i >= n_prompt_tiles)
    def _():
        emit(xs_ref)


def _prenorm(x_prompt, x_sample, g):
    n_prompt, d = x_prompt.shape
    tm = x_sample.shape[0]
    assert n_prompt % tm == 0 and tm % (2 * SUBLANES) == 0
    npt = n_prompt // tm
    t = n_prompt + tm
    row = pl.BlockSpec((tm, d), lambda i: (i, 0))
    return pl.pallas_call(
        functools.partial(_prenorm_kernel, n_prompt_tiles=npt),
        grid=(npt + 1,),
        in_specs=[pl.BlockSpec((tm, d), lambda i: (jnp.minimum(i, npt - 1), 0)),
                  pl.BlockSpec((tm, d), lambda i: (0, 0)), pl.BlockSpec((1, d), lambda i: (0, 0))],
        out_specs=[row, row],
        out_shape=[jax.ShapeDtypeStruct((t, d), F32), jax.ShapeDtypeStruct((t, d), BF16)],
        compiler_params=_cp("parallel"),
        name="prenorm",
    )(x_prompt, x_sample, g.reshape(1, d))


def _post_kernel(x_ref, y_ref, gp_ref, gn_ref, xo_ref, ho_ref, *, scale):
    xn = x_ref[...] + scale * _rms(y_ref[...], gp_ref[...])
    xo_ref[...] = xn
    ho_ref[...] = _rms(xn, gn_ref[...]).astype(ho_ref.dtype)


def _post_last_kernel(x_ref, y_ref, gp_ref, xp_ref, xs_ref, *, scale, n_prompt_tiles):
    i = pl.program_id(0)
    xn = x_ref[...] + scale * _rms(y_ref[...], gp_ref[...])

    @pl.when(i < n_prompt_tiles)
    def _():
        xp_ref[...] = xn

    @pl.when(i >= n_prompt_tiles)
    def _():
        xs_ref[...] = xn


def _post_last(x, y, g_post, scale, n_prompt):
    t, d = x.shape
    tm = t - n_prompt
    assert n_prompt % tm == 0 and tm % SUBLANES == 0
    npt = n_prompt // tm
    row = pl.BlockSpec((tm, d), lambda i: (i, 0))
    vec = pl.BlockSpec((1, d), lambda i: (0, 0))
    return pl.pallas_call(
        functools.partial(_post_last_kernel, scale=scale, n_prompt_tiles=npt),
        grid=(t // tm,),
        in_specs=[row, row, vec],
        out_specs=[pl.BlockSpec((tm, d), lambda i: (jnp.minimum(i, npt - 1), 0)),
                   pl.BlockSpec((tm, d), lambda i: (0, 0))],
        out_shape=[jax.ShapeDtypeStruct((n_prompt, d), F32), jax.ShapeDtypeStruct((tm, d), F32)],
        compiler_params=_cp("arbitrary"),
        name="post_last",
    )(x, y, g_post.reshape(1, d))


def _post(x, y, g_post, scale, g_next):
    t, d = x.shape
    tm = _tile(t, (208, 128, 64, 32, 16))
    row = pl.BlockSpec((tm, d), lambda i: (i, 0))
    vec = pl.BlockSpec((1, d), lambda i: (0, 0))
    return pl.pallas_call(
        functools.partial(_post_kernel, scale=scale),
        grid=(t // tm,),
        in_specs=[row, row, vec, vec],
        out_specs=[row, row],
        out_shape=[jax.ShapeDtypeStruct((t, d), F32), jax.ShapeDtypeStruct((t, d), BF16)],
        compiler_params=_cp("parallel"),
        name="post",
    )(x, y, g_post.reshape(1, d), g_next.reshape(1, d))


def _ffn_kernel(h_ref, wg_ref, wu_ref, wd_ref, o_ref, *, n_chunks):
    j = pl.program_id(1)

    @pl.when(j == 0)
    def _():
        o_ref[...] = jnp.zeros(o_ref.shape, o_ref.dtype)

    h = h_ref[...]
    g = _dot(h, wg_ref[...].astype(BF16))
    u = _dot(h, wu_ref[...].astype(BF16))
    a = (g * jax.nn.sigmoid(g) * u).astype(BF16)
    cw = o_ref.shape[1] // n_chunks
    for c in range(n_chunks):
        o_ref[:, c * cw:(c + 1) * cw] += _dot(a, wd_ref[:, c * cw:(c + 1) * cw].astype(BF16))


def _ffn(h, wg, wu, wd, layer, half):
    t, d = h.shape
    f = wg.shape[-1]
    tm = _tile(t, (1040, 512, 256, 128, 64, 32, 16))
    tf = _tile(f, (256, 128))
    n_chunks = max(1, d // 512)
    return pl.pallas_call(
        functools.partial(_ffn_kernel, n_chunks=n_chunks),
        grid=(t // tm, f // tf),
        in_specs=[
            pl.BlockSpec((tm, d), lambda i, j: (i, 0), pipeline_mode=pl.Buffered(1)),
            pl.BlockSpec((None, None, d, tf), lambda i, j: (layer, half, 0, j)),
            pl.BlockSpec((None, None, d, tf), lambda i, j: (layer, half, 0, j)),
            pl.BlockSpec((None, None, tf, d), lambda i, j: (layer, half, j, 0)),
        ],
        out_specs=pl.BlockSpec((tm, d), lambda i, j: (i, 0), pipeline_mode=pl.Buffered(1)),
        out_shape=jax.ShapeDtypeStruct((t, d), F32),
        compiler_params=pltpu.CompilerParams(dimension_semantics=("parallel", "arbitrary"),
                                             vmem_limit_bytes=FFN_VMEM_LIMIT_BYTES),
        name="ffn",
    )(h, wg, wu, wd)


def _mm_kernel(x_ref, w_ref, o_ref):
    o_ref[...] = _dot(x_ref[...], w_ref[...].astype(BF16)).astype(o_ref.dtype)


def _mm_specs(t, k, n):
    tm = _tile(t, (1040, 512, 256, 128, 64, 32, 16))
    tn = _tile(n, (512, 256, 128))
    return tm, tn


def _mm(x, w):
    t, k = x.shape
    n = w.shape[1]
    tm, tn = _mm_specs(t, k, n)
    return pl.pallas_call(
        _mm_kernel,
        grid=(t // tm, n // tn),
        in_specs=[pl.BlockSpec((tm, k), lambda i, j: (i, 0)), pl.BlockSpec((k, tn), lambda i, j: (0, j))],
        out_specs=pl.BlockSpec((tm, tn), lambda i, j: (i, j)),
        out_shape=jax.ShapeDtypeStruct((t, n), F32),
        compiler_params=_cp("parallel", "arbitrary"),
        name="mm",
    )(x, w)


def _mm2_kernel(x1_ref, x2_ref, w_ref, o_ref):
    k1 = x1_ref.shape[1]
    o_ref[...] = (_dot(x1_ref[...], w_ref[:k1, :].astype(BF16))
                  + _dot(x2_ref[...], w_ref[k1:, :].astype(BF16)))


def _mm2(x1, x2, w):
    t, k1 = x1.shape
    k2 = x2.shape[1]
    n = w.shape[1]
    tm, tn = _mm_specs(t, k1 + k2, n)
    return pl.pallas_call(
        _mm2_kernel,
        grid=(t // tm, n // tn),
        in_specs=[pl.BlockSpec((tm, k1), lambda i, j: (i, 0)), pl.BlockSpec((tm, k2), lambda i, j: (i, 0)),
                  pl.BlockSpec((k1 + k2, tn), lambda i, j: (0, j))],
        out_specs=pl.BlockSpec((tm, tn), lambda i, j: (i, j)),
        out_shape=jax.ShapeDtypeStruct((t, n), F32),
        compiler_params=_cp("parallel", "arbitrary"),
        name="mm2",
    )(x1, x2, w)


def _mm_glu_kernel(zb_ref, w_ref, z_ref, o_ref):
    s = _dot(zb_ref[...], w_ref[...].astype(BF16))
    o_ref[...] = (z_ref[...] * jax.nn.sigmoid(s)).astype(o_ref.dtype)


def _mm_glu(zb, w, z):
    t, k = zb.shape
    n = w.shape[1]
    tm, tn = _mm_specs(t, k, n)
    return pl.pallas_call(
        _mm_glu_kernel,
        grid=(t // tm, n // tn),
        in_specs=[pl.BlockSpec((tm, k), lambda i, j: (i, 0)), pl.BlockSpec((k, tn), lambda i, j: (0, j)),
                  pl.BlockSpec((tm, tn), lambda i, j: (i, j))],
        out_specs=pl.BlockSpec((tm, tn), lambda i, j: (i, j)),
        out_shape=jax.ShapeDtypeStruct((t, n), BF16),
        compiler_params=_cp("parallel", "arbitrary"),
        name="mm_glu",
    )(zb, w, z)


def _pool_prompt_kernel(u_ref, wp_ref, ps_ref, o_ref):
    s_len = u_ref.shape[0]
    pg = wp_ref.shape[1]
    row = lax.broadcasted_iota(jnp.int32, (s_len, pg), 0)
    for gi, w in enumerate(POOL_WINDOWS):
        cols = slice(gi * pg, (gi + 1) * pg)
        u = u_ref[:, cols]
        s = u
        d = 1
        while d < w:
            s = s + jnp.where(row >= d, pltpu.roll(s, d, axis=0), 0.0)
            d *= 2
        cnt = jnp.minimum(row + 1, w).astype(F32)
        dd = (s / cnt - u).astype(BF16)
        o_ref[:, cols] = (_dot(dd, wp_ref[gi]) * ps_ref[:, cols]).astype(o_ref.dtype)


def _pool_prompt(u_pool, w_pool, pool_scale, batch, seq):
    c = u_pool.shape[1]
    return pl.pallas_call(
        _pool_prompt_kernel,
        grid=(batch,),
        in_specs=[pl.BlockSpec((seq, c), lambda b: (b, 0)),
                  pl.BlockSpec(w_pool.shape, lambda b: (0, 0, 0)),
                  pl.BlockSpec((1, c), lambda b: (0, 0))],
        out_specs=pl.BlockSpec((seq, c), lambda b: (b, 0)),
        out_shape=jax.ShapeDtypeStruct((u_pool.shape[0], c), BF16),
        compiler_params=_cp("parallel"),
        name="pool_prompt",
    )(u_pool, w_pool, pool_scale.reshape(1, c))


def _pool_sample_kernel(ext_ref, wp_ref, ps_ref, o_in_ref, o_ref, *, start):
    del o_in_ref
    nb = ext_ref.shape[0]
    pg = wp_ref.shape[1]
    for gi, w in enumerate(POOL_WINDOWS):
        cols = slice(gi * pg, (gi + 1) * pg)
        u = ext_ref[nb - 1, :, cols]
        s = u
        for i in range(1, w):
            s = s + ext_ref[nb - 1 - i, :, cols]
        dd = (s / float(min(start + 1, w)) - u).astype(BF16)
        o_ref[:, cols] = (_dot(dd, wp_ref[gi]) * ps_ref[:, cols]).astype(o_ref.dtype)


def _pool_sample(ext_t, w_pool, pool_scale, start, y_full, row_off):
    nb, bd, c = ext_t.shape
    assert row_off % bd == 0
    rb = row_off // bd
    return pl.pallas_call(
        functools.partial(_pool_sample_kernel, start=start),
        grid=(1,),
        in_specs=[pl.BlockSpec((nb, bd, c), lambda i: (0, 0, 0)),
                  pl.BlockSpec(w_pool.shape, lambda i: (0, 0, 0)),
                  pl.BlockSpec((1, c), lambda i: (0, 0)),
                  pl.BlockSpec(memory_space=pl.ANY)],
        out_specs=pl.BlockSpec((bd, c), lambda i: (rb, 0)),
        out_shape=jax.ShapeDtypeStruct(y_full.shape, y_full.dtype),
        input_output_aliases={3: 0},
        compiler_params=_cp("arbitrary"),
        name="pool_sample",
    )(ext_t, w_pool, pool_scale.reshape(1, c), y_full)


def _mla_prep_kernel(h_ref, wp_ref, wq_ref, wkv_ref, wkp_ref, wkps_ref, qn_ref, kvn_ref, cos_ref, sin_ref,
                     up_ref, cq_ref, ckv_ref, ckvb_ref, kpe_ref, kpeb_ref):
    h = h_ref[...]
    up_ref[...] = _dot(h, wp_ref[...].astype(BF16))
    cq_ref[...] = _rms(_dot(h, wq_ref[...].astype(BF16)), qn_ref[...]).astype(cq_ref.dtype)
    ckv = _rms(_dot(h, wkv_ref[...].astype(BF16)), kvn_ref[...])
    ckv_ref[...] = ckv
    ckvb_ref[...] = ckv.astype(ckvb_ref.dtype)
    kpe = _dot(h, wkp_ref[...]) * cos_ref[...] + _dot(h, wkps_ref[...]) * sin_ref[...]
    kpe_ref[...] = kpe
    kpeb_ref[...] = kpe.astype(kpeb_ref.dtype)


def _mla_prep(h, w_in0, o_q, o_kv, w_kpe, w_kpe_sw, q_norm, kv_norm, cosf, sinf):
    t, d = h.shape
    ql, kl, r = q_norm.shape[0], kv_norm.shape[0], w_kpe.shape[1]
    pw = o_q
    assert o_q % ql == 0 and o_kv % kl == 0
    tm = _tile(t, (208, 128, 64, 32, 16))
    row = lambda n: pl.BlockSpec((tm, n), lambda i: (i, 0))
    full = lambda a, b: pl.BlockSpec((a, b), lambda i: (0, 0))
    win = lambda n, off: pl.BlockSpec((d, n), lambda i: (0, off // n), pipeline_mode=pl.Buffered(1))
    return pl.pallas_call(
        _mla_prep_kernel,
        grid=(t // tm,),
        in_specs=[row(d), win(pw, 0), win(ql, o_q), win(kl, o_kv), full(d, r), full(d, r), full(1, ql), full(1, kl),
                  row(r), row(r)],
        out_specs=[row(pw), row(ql), row(kl), row(kl), row(r), row(r)],
        out_shape=[jax.ShapeDtypeStruct((t, pw), F32), jax.ShapeDtypeStruct((t, ql), BF16),
                   jax.ShapeDtypeStruct((t, kl), F32), jax.ShapeDtypeStruct((t, kl), BF16),
                   jax.ShapeDtypeStruct((t, r), F32), jax.ShapeDtypeStruct((t, r), BF16)],
        compiler_params=_cp("parallel"),
        name="mla_prep",
    )(h, w_in0, w_in0, w_in0, w_kpe, w_kpe_sw, q_norm.reshape(1, ql), kv_norm.reshape(1, kl), cosf, sinf)


def _q_prep_kernel(cq_ref, wqn_ref, wqr_ref, wqrs_ref, wuk_ref, cos_ref, sin_ref, ql_ref, qp_ref):
    cq = cq_ref[...]
    hp, nope, _ = wuk_ref.shape
    r = qp_ref.shape[2]
    q_nope = _dot(cq, wqn_ref[0]).astype(BF16)
    q_pe = _dot(cq, wqr_ref[0]) * cos_ref[...] + _dot(cq, wqrs_ref[0]) * sin_ref[...]
    for hh in range(hp):
        ql_ref[hh] = _dot(q_nope[:, hh * nope:(hh + 1) * nope], wuk_ref[hh]).astype(ql_ref.dtype)
        qp_ref[hh] = q_pe[:, hh * r:(hh + 1) * r].astype(qp_ref.dtype)


def _group_heads(w, hp):
    heads, a, b = w.shape
    return w.reshape(heads // hp, hp, a, b).transpose(0, 2, 1, 3).reshape(heads // hp, a, hp * b)


def _q_prep(cq, w_qn, w_qr, w_qr_sw, w_uk_t, cosf, sinf):
    t, ql = cq.shape
    heads, _, nope = w_qn.shape
    r = w_qr.shape[2]
    kl = w_uk_t.shape[2]
    hp = _tile(heads, (4, 2, 1))
    tm = _tile(t, (1040, 512, 256, 128, 64, 32, 16))
    hw = lambda a, b: pl.BlockSpec((1, a, b), lambda i, h: (h, 0, 0))
    rows = lambda n: pl.BlockSpec((tm, n), lambda i, h: (i, 0))
    return pl.pallas_call(
        _q_prep_kernel,
        grid=(t // tm, heads // hp),
        in_specs=[rows(ql), hw(ql, hp * nope), hw(ql, hp * r), hw(ql, hp * r),
                  pl.BlockSpec((hp, nope, kl), lambda i, h: (h, 0, 0)), rows(hp * r), rows(hp * r)],
        out_specs=[pl.BlockSpec((hp, tm, kl), lambda i, h: (h, i, 0)),
                   pl.BlockSpec((hp, tm, r), lambda i, h: (h, i, 0))],
        out_shape=[jax.ShapeDtypeStruct((heads, t, kl), BF16), jax.ShapeDtypeStruct((heads, t, r), BF16)],
        compiler_params=_cp("parallel", "arbitrary"),
        name="q_prep",
    )(cq, _group_heads(w_qn, hp), _group_heads(w_qr, hp), _group_heads(w_qr_sw, hp), w_uk_t,
      jnp.tile(cosf, (1, hp)), jnp.tile(sinf, (1, hp)))


def _o_proj_kernel(ol_ref, wuv_ref, o_in_ref, o_ref):
    del o_in_ref
    o_ref[...] = _dot(ol_ref[0], wuv_ref[0]).astype(o_ref.dtype)


def _o_proj(o_lat, w_uv_t, o_full, row_off):
    heads, n, kl = o_lat.shape
    vh = w_uv_t.shape[2]
    assert row_off % n == 0
    rb = row_off // n
    return pl.pallas_call(
        _o_proj_kernel,
        grid=(heads,),
        in_specs=[pl.BlockSpec((1, n, kl), lambda h: (h, 0, 0)), pl.BlockSpec((1, kl, vh), lambda h: (h, 0, 0)),
                  pl.BlockSpec(memory_space=pl.ANY)],
        out_specs=pl.BlockSpec((n, vh), lambda h: (rb, h)),
        out_shape=jax.ShapeDtypeStruct(o_full.shape, o_full.dtype),
        input_output_aliases={2: 0},
        compiler_params=_cp("arbitrary"),
        name="o_proj",
    )(o_lat, w_uv_t, o_full)


def _softmax_update(s, v, m_ref, l_ref, acc_ref):
    m_prev = m_ref[...]
    m_new = jnp.maximum(m_prev, jnp.max(s, axis=1, keepdims=True))
    alpha = jnp.exp(m_prev - m_new)
    p = jnp.exp(s - _rep(m_new, s.shape[1] // LANES))
    l_ref[...] = alpha * l_ref[...] + jnp.sum(p, axis=1, keepdims=True)
    acc_ref[...] = acc_ref[...] * _rep(alpha, acc_ref.shape[1] // LANES) + _dot(p.astype(BF16), v)
    m_ref[...] = m_new


def _attn_prompt_kernel(ql_ref, qp_ref, k_ref, kp_ref, wuv_ref, o_ref, m_ref, l_ref, acc_ref, *, scale):
    qi = pl.program_id(1)
    hg, tq, c = ql_ref.shape
    r = hg * tq
    q = ql_ref[...].reshape(r, c)
    qp = qp_ref[...].reshape(r, qp_ref.shape[2])
    m_ref[...] = jnp.full(m_ref.shape, NEG_INF, F32)
    l_ref[...] = jnp.zeros(l_ref.shape, F32)
    acc_ref[...] = jnp.zeros(acc_ref.shape, F32)

    def block(start, width, masked):
        k = k_ref[pl.ds(start, width), :]
        kp = kp_ref[pl.ds(start, width), :]
        s = (_dot_nt(q, k) + _dot_nt(qp, kp)) * scale
        if masked:
            s3 = s.reshape(hg, tq, width)
            qpos = qi * tq + lax.broadcasted_iota(jnp.int32, s3.shape, 1)
            kpos = start + lax.broadcasted_iota(jnp.int32, s3.shape, 2)
            s = jnp.where(kpos <= qpos, s3, NEG_INF).reshape(r, width)
        _softmax_update(s, k, m_ref, l_ref, acc_ref)

    def body(kb, carry):
        block(pl.multiple_of(kb * (2 * tq), 2 * tq), 2 * tq, False)
        return carry

    lax.fori_loop(0, qi // 2, body, 0)

    @pl.when(qi % 2 == 0)
    def _():
        block(pl.multiple_of(qi * tq, tq), tq, True)

    @pl.when(qi % 2 == 1)
    def _():
        block(pl.multiple_of((qi - 1) * tq, 2 * tq), 2 * tq, True)

    o = (acc_ref[...] / _rep(l_ref[...], c // LANES)).astype(BF16)
    vh = wuv_ref.shape[2]
    for hh in range(hg):
        o_ref[:, hh * vh:(hh + 1) * vh] = _dot(o[hh * tq:(hh + 1) * tq, :], wuv_ref[hh]).astype(o_ref.dtype)


def _attn_prompt(q_lat, q_pe, ckv_b, kpe_b, w_uv_t, batch, seq, scale):
    heads, n_tok, c = q_lat.shape
    r = q_pe.shape[2]
    vh = w_uv_t.shape[2]
    tq = _tile(seq, (256, 128))
    hg = _tile(heads, (12, 8, 4, 2, 1))
    nq = seq // tq
    rows = hg * tq
    return pl.pallas_call(
        functools.partial(_attn_prompt_kernel, scale=scale),
        grid=(batch, nq, heads // hg),
        in_specs=[pl.BlockSpec((hg, tq, c), lambda b, i, g: (g, b * nq + i, 0)),
                  pl.BlockSpec((hg, tq, r), lambda b, i, g: (g, b * nq + i, 0)),
                  pl.BlockSpec((seq, c), lambda b, i, g: (b, 0)),
                  pl.BlockSpec((seq, r), lambda b, i, g: (b, 0)),
                  pl.BlockSpec((hg, c, vh), lambda b, i, g: (g, 0, 0))],
        out_specs=pl.BlockSpec((tq, hg * vh), lambda b, i, g: (b * nq + i, g)),
        out_shape=jax.ShapeDtypeStruct((n_tok, heads * vh), BF16),
        scratch_shapes=[pltpu.VMEM((rows, LANES), F32), pltpu.VMEM((rows, LANES), F32),
                        pltpu.VMEM((rows, c), F32)],
        compiler_params=_cp("parallel", "parallel", "arbitrary"),
        name="attn_prompt",
    )(q_lat, q_pe, ckv_b, kpe_b, w_uv_t)


def _attn_sample_kernel(pt_ref, ql_ref, qp_ref, cn_ref, pn_ref, ck_hbm, kp_hbm, o_ref,
                        kbuf, pbuf, sem, m_ref, l_ref, acc_ref, *, pps, scale):
    n_slots = kbuf.shape[0]
    ahead = n_slots - 1
    n_steps = pl.num_programs(1)
    step = pl.program_id(1)
    gstep = pl.program_id(0) * n_steps + step
    total = pl.num_programs(0) * n_steps
    slot = gstep % n_slots

    def page_copies(g, slot_idx, i):
        pid = pt_ref[g * pps + i]
        return (pltpu.make_async_copy(ck_hbm.at[pid], kbuf.at[slot_idx, i], sem.at[0, slot_idx]),
                pltpu.make_async_copy(kp_hbm.at[pid], pbuf.at[slot_idx, i], sem.at[1, slot_idx]))

    def start_fetch(g, slot_idx):
        for i in range(pps):
            for cp in page_copies(g, slot_idx, i):
                cp.start()

    @pl.when(gstep == 0)
    def _():
        for g in range(ahead):
            start_fetch(g, g)

    @pl.when(gstep + ahead < total)
    def _():
        start_fetch(gstep + ahead, (gstep + ahead) % n_slots)

    for i in range(pps):
        for cp in page_copies(gstep, slot, i):
            cp.wait()

    q = ql_ref[0]
    qp = qp_ref[0]

    @pl.when(step == 0)
    def _():
        cn = cn_ref[0]
        pn = pn_ref[0]
        s0 = (jnp.sum(q.astype(F32) * cn, axis=1, keepdims=True)
              + jnp.sum(qp.astype(F32) * pn, axis=1, keepdims=True)) * scale
        m_ref[...] = jnp.broadcast_to(s0, m_ref.shape)
        l_ref[...] = jnp.ones(l_ref.shape, F32)
        acc_ref[...] = jnp.broadcast_to(cn, acc_ref.shape)

    page, c = kbuf.shape[2], kbuf.shape[3]
    k = kbuf[slot].reshape(pps * page, c).astype(BF16)
    kp_t = jnp.concatenate([pbuf[slot, i].astype(BF16) for i in range(pps)], axis=1)
    s = (_dot_nt(q, k) + _dot(qp, kp_t)) * scale
    _softmax_update(s, k, m_ref, l_ref, acc_ref)

    @pl.when(step == pl.num_programs(1) - 1)
    def _():
        o_ref[0] = (acc_ref[...] / _rep(l_ref[...], acc_ref.shape[1] // LANES)).astype(o_ref.dtype)


def _attn_sample(q_lat, q_pe, ckv_new, kpe_new, cache_ckv, cache_kpe_t, page_table, scale):
    bd, heads, c = q_lat.shape
    r = q_pe.shape[2]
    n_pages = page_table.shape[1]
    page = cache_ckv.shape[1]
    pps = _tile(n_pages, (32, 16, 8, 4, 2, 1))
    assert bd * (n_pages // pps) >= PAGE_RING_SLOTS - 1

    per_seq = lambda n, w: pl.BlockSpec((1, n, w), lambda b, s, pt: (b, 0, 0))
    hbm = pl.BlockSpec(memory_space=pl.ANY)
    return pl.pallas_call(
        functools.partial(_attn_sample_kernel, pps=pps, scale=scale),
        grid_spec=pltpu.PrefetchScalarGridSpec(
            num_scalar_prefetch=1,
            grid=(bd, n_pages // pps),
            in_specs=[per_seq(heads, c), per_seq(heads, r), per_seq(1, c), per_seq(1, r), hbm, hbm],
            out_specs=per_seq(heads, c),
            scratch_shapes=[pltpu.VMEM((PAGE_RING_SLOTS, pps, page, c), F32),
                            pltpu.VMEM((PAGE_RING_SLOTS, pps, r, page), F32),
                            pltpu.SemaphoreType.DMA((2, PAGE_RING_SLOTS)),
                            pltpu.VMEM((heads, LANES), F32), pltpu.VMEM((heads, LANES), F32),
                            pltpu.VMEM((heads, c), F32)],
        ),
        out_shape=jax.ShapeDtypeStruct((bd, heads, c), BF16),
        compiler_params=_cp("arbitrary", "arbitrary"),
        name="attn_sample",
    )(page_table.reshape(-1), q_lat, q_pe, ckv_new, kpe_new, cache_ckv, cache_kpe_t)


def _s5_disc_kernel(lr_ref, li_ref, ldt_ref, br_ref, bi_ref, bbr_ref, bbi_ref, ar_ref, ai_ref):
    dt = jnp.exp(ldt_ref[...])
    lr = lr_ref[...]
    li = li_ref[...]
    mag = jnp.exp(lr * dt)
    a_re = mag * jnp.cos(li * dt)
    a_im = mag * jnp.sin(li * dt)
    den = lr * lr + li * li
    nr, ni = a_re - 1.0, a_im
    f_re = (nr * lr + ni * li) / den
    f_im = (ni * lr - nr * li) / den
    for k in range(br_ref.shape[0]):
        bbr_ref[k] = f_re * br_ref[k] - f_im * bi_ref[k]
        bbi_ref[k] = f_re * bi_ref[k] + f_im * br_ref[k]
    ar_ref[...] = a_re
    ai_ref[...] = a_im


def _s5_disc(lam_re, lam_im, log_dt, b_re_t, b_im_t):
    k, g, n = b_re_t.shape
    spec2 = pl.BlockSpec((g, n), lambda i: (0, 0))
    spec3 = lambda a: pl.BlockSpec((a, g, n), lambda i: (0, 0, 0))
    return pl.pallas_call(
        _s5_disc_kernel,
        grid=(1,),
        in_specs=[spec2, spec2, spec2, spec3(k), spec3(k)],
        out_specs=[spec3(k), spec3(k), spec2, spec2],
        out_shape=[jax.ShapeDtypeStruct((k, g, n), F32)] * 2 + [jax.ShapeDtypeStruct((g, n), F32)] * 2,
        compiler_params=_cp("arbitrary"),
        name="s5_disc",
    )(lam_re, lam_im, jnp.broadcast_to(log_dt[:, None], (g, n)), b_re_t, b_im_t)


def _idiv(x, d):
    return x >> (d.bit_length() - 1) if d & (d - 1) == 0 else x // d


def _expand_block_diag(dd, grp, n_state, cw_st):
    tiled = _rep(dd, cw_st // LANES)
    r = lax.broadcasted_iota(jnp.int32, tiled.shape, 0)
    c = lax.broadcasted_iota(jnp.int32, tiled.shape, 1)
    return jnp.where(_idiv(r, grp) == _idiv(c, n_state), tiled, jnp.zeros_like(tiled))


def _s5_scan_kernel(u_ref, bdr_ref, bdi_ref, pr_ref, pi_ref, cdr_ref, cdi_ref, d_ref, h0r_ref, h0i_ref,
                    z_ref, zb_ref, hr_out_ref, hi_out_ref,
                    bbr_ref, bbi_ref, cmr_ref, cmi_ref,
                    st_ref, up_ref, sr_ref, si_ref, ir_ref, ii_ref, cr_carry, ci_carry, *, grp, n_state):
    tc = pl.program_id(2)
    tt, w = sr_ref.shape
    nl = tt // SUBLANES
    assert nl & (nl - 1) == 0, "sub-block length must be a power of two (A_bar^nl by squaring)"

    @pl.when(tc == 0)
    def _():
        cr_carry[...] = h0r_ref[0, 0]
        ci_carry[...] = h0i_ref[0, 0]
        bbr_ref[...] = _expand_block_diag(bdr_ref[...], grp, n_state, w)
        bbi_ref[...] = _expand_block_diag(bdi_ref[...], grp, n_state, w)
        cmr_ref[...] = _expand_block_diag(cdr_ref[...], grp, n_state, w)
        cmi_ref[...] = _expand_block_diag(cdi_ref[...], grp, n_state, w)

    n_lt = u_ref.shape[1] // LANES
    pitch = st_ref.shape[1] // SUBLANES
    for s in range(SUBLANES):
        for j in range(n_lt):
            st_ref[j, s * pitch:s * pitch + nl, :] = u_ref[s * nl:(s + 1) * nl, j * LANES:(j + 1) * LANES]
    for i in range(nl):
        for j in range(n_lt):
            up_ref[i * SUBLANES:(i + 1) * SUBLANES, j * LANES:(j + 1) * LANES] = (
                st_ref[j, pl.ds(i, SUBLANES, stride=pitch), :])
    ub = up_ref[...].astype(BF16)
    sr_ref[...] = _dot(ub, bbr_ref[...])
    si_ref[...] = _dot(ub, bbi_ref[...])
    a_re = pr_ref[0]
    a_im = pi_ref[0]
    a_re8 = jnp.broadcast_to(a_re, (SUBLANES, w))
    a_im8 = jnp.broadcast_to(a_im, (SUBLANES, w))

    def pass1(i, carry):
        h_re, h_im = carry
        off = pl.multiple_of(i * SUBLANES, SUBLANES)
        x_re = sr_ref[pl.ds(off, SUBLANES), :]
        x_im = si_ref[pl.ds(off, SUBLANES), :]
        h_re, h_im = a_re8 * h_re - a_im8 * h_im + x_re, a_re8 * h_im + a_im8 * h_re + x_im
        sr_ref[pl.ds(off, SUBLANES), :] = h_re
        si_ref[pl.ds(off, SUBLANES), :] = h_im
        return h_re, h_im

    zero = jnp.zeros((SUBLANES, w), F32)
    f_re, f_im = lax.fori_loop(0, nl, pass1, (zero, zero))

    al_re, al_im = a_re, a_im
    n = nl
    while n > 1:
        al_re, al_im = al_re * al_re - al_im * al_im, 2.0 * (al_re * al_im)
        n //= 2
    c_re, c_im = cr_carry[...], ci_carry[...]
    for s in range(SUBLANES):
        ir_ref[s:s + 1, :] = c_re
        ii_ref[s:s + 1, :] = c_im
        c_re, c_im = (f_re[s:s + 1, :] + al_re * c_re - al_im * c_im,
                      f_im[s:s + 1, :] + al_re * c_im + al_im * c_re)
    cr_carry[...] = c_re
    ci_carry[...] = c_im
    i_re = ir_ref[...]
    i_im = ii_ref[...]

    def pass2(i, carry):
        p_re, p_im = carry
        off = pl.multiple_of(i * SUBLANES, SUBLANES)
        p_re8 = jnp.broadcast_to(p_re, (SUBLANES, w))
        p_im8 = jnp.broadcast_to(p_im, (SUBLANES, w))
        sr_ref[pl.ds(off, SUBLANES), :] = sr_ref[pl.ds(off, SUBLANES), :] + (p_re8 * i_re - p_im8 * i_im)
        si_ref[pl.ds(off, SUBLANES), :] = si_ref[pl.ds(off, SUBLANES), :] + (p_re8 * i_im + p_im8 * i_re)
        return p_re * a_re - p_im * a_im, p_re * a_im + p_im * a_re

    lax.fori_loop(0, nl, pass2, (a_re, a_im))

    y = _dot_nt(sr_ref[...].astype(BF16), cmr_ref[...]) - _dot_nt(si_ref[...].astype(BF16), cmi_ref[...])
    for i in range(nl):
        for j in range(n_lt):
            st_ref[j, pl.ds(i, SUBLANES, stride=pitch), :] = (
                y[i * SUBLANES:(i + 1) * SUBLANES, j * LANES:(j + 1) * LANES])
    for s in range(SUBLANES):
        rows = slice(s * nl, (s + 1) * nl)
        for j in range(n_lt):
            cols = slice(j * LANES, (j + 1) * LANES)
            z = jax.nn.gelu(st_ref[j, s * pitch:s * pitch + nl, :] + d_ref[:, cols] * u_ref[rows, cols],
                            approximate=True)
            z_ref[rows, cols] = z
            zb_ref[rows, cols] = z.astype(zb_ref.dtype)

    @pl.when(tc == pl.num_programs(2) - 1)
    def _():
        hr_out_ref[0, 0] = c_re
        hi_out_ref[0, 0] = c_im


def _s5_scan(u, row_off, batch, seq, grp, n_state, bd_re, bd_im, pw_re, pw_im, cd_re, cd_im, d_skip, h0_re, h0_im):
    n_chunks, _, cw_st = pw_re.shape
    cw_in = cw_st // n_state * grp
    tt = _tile(seq, (512, 256, 128, 64, 32, 16, 8))
    assert seq % tt == 0 and row_off % tt == 0
    nt = seq // tt
    ob = row_off // tt
    cmat = lambda a, b: pl.BlockSpec((1, a, b), lambda b_, c, t: (c, 0, 0))
    dmat = pl.BlockSpec((cw_in, LANES), lambda b_, c, t: (c, 0))
    st = pl.BlockSpec((1, 1, 1, cw_st), lambda b_, c, t: (b_, c, 0, 0))
    urow = pl.BlockSpec((tt, cw_in), lambda b_, c, t: (ob + b_ * nt + t, c))
    orow = urow
    rows = u.shape[0]
    wd = n_chunks * cw_in
    return pl.pallas_call(
        functools.partial(_s5_scan_kernel, grp=grp, n_state=n_state),
        grid=(batch, n_chunks, nt),
        in_specs=[urow, dmat, dmat, cmat(1, cw_st), cmat(1, cw_st),
                  dmat, dmat, pl.BlockSpec((1, cw_in), lambda b_, c, t: (0, c)), st, st],
        out_specs=[orow, orow, st, st],
        out_shape=[jax.ShapeDtypeStruct((rows, wd), F32), jax.ShapeDtypeStruct((rows, wd), BF16),
                   jax.ShapeDtypeStruct(h0_re.shape, F32), jax.ShapeDtypeStruct(h0_im.shape, F32)],
        scratch_shapes=[pltpu.VMEM((cw_in, cw_st), BF16), pltpu.VMEM((cw_in, cw_st), BF16),
                        pltpu.VMEM((cw_in, cw_st), BF16), pltpu.VMEM((cw_in, cw_st), BF16),
                        pltpu.VMEM((cw_in // LANES, tt + SUBLANES * SUBLANES, LANES), F32),
                        pltpu.VMEM((tt, cw_in), F32),
                        pltpu.VMEM((tt, cw_st), F32), pltpu.VMEM((tt, cw_st), F32),
                        pltpu.VMEM((SUBLANES, cw_st), F32), pltpu.VMEM((SUBLANES, cw_st), F32),
                        pltpu.VMEM((1, cw_st), F32), pltpu.VMEM((1, cw_st), F32)],
        compiler_params=_cp("parallel", "parallel", "arbitrary"),
        name="s5_scan",
    )(u, bd_re, bd_im, pw_re, pw_im, cd_re, cd_im, d_skip.reshape(1, wd), h0_re, h0_im)


def _s5_step_kernel(u_ref, bdr_ref, bdi_ref, pr_ref, pi_ref, cdr_ref, cdi_ref, d_ref, h0r_ref, h0i_ref,
                    z_in_ref, zb_in_ref, z_ref, zb_ref, hr_out_ref, hi_out_ref, *, grp, n_state):
    del z_in_ref, zb_in_ref
    w = h0r_ref.shape[1]
    u = u_ref[...]
    ub = u.astype(BF16)
    a_re = pr_ref[0]
    a_im = pi_ref[0]
    h0r = h0r_ref[...]
    h0i = h0i_ref[...]
    h_re = _dot(ub, _expand_block_diag(bdr_ref[...], grp, n_state, w)) + (a_re * h0r - a_im * h0i)
    h_im = _dot(ub, _expand_block_diag(bdi_ref[...], grp, n_state, w)) + (a_re * h0i + a_im * h0r)
    hr_out_ref[...] = h_re
    hi_out_ref[...] = h_im
    y = (_dot_nt(h_re.astype(BF16), _expand_block_diag(cdr_ref[...], grp, n_state, w))
         - _dot_nt(h_im.astype(BF16), _expand_block_diag(cdi_ref[...], grp, n_state, w)) + d_ref[...] * u)
    z = jax.nn.gelu(y, approximate=True)
    z_ref[...] = z
    zb_ref[...] = z.astype(zb_ref.dtype)


def _s5_step(u, row_off, n_seq, grp, n_state, bd_re, bd_im, pw_re, pw_im, cd_re, cd_im, d_skip, h0_re, h0_im,
             z, zb):
    n_chunks, _, cw_st = pw_re.shape
    cw_in = cw_st // n_state * grp
    ob = row_off // n_seq
    cmat = lambda a, b: pl.BlockSpec((1, a, b), lambda c: (c, 0, 0))
    dmat = pl.BlockSpec((cw_in, LANES), lambda c: (c, 0))
    st = pl.BlockSpec((n_seq, cw_st), lambda c: (0, c))
    urow = pl.BlockSpec((n_seq, cw_in), lambda c: (ob, c))
    wd = n_chunks * cw_in
    anyspec = pl.BlockSpec(memory_space=pl.ANY)
    return pl.pallas_call(
        functools.partial(_s5_step_kernel, grp=grp, n_state=n_state),
        grid=(n_chunks,),
        in_specs=[urow, dmat, dmat, cmat(1, cw_st), cmat(1, cw_st), dmat, dmat,
                  pl.BlockSpec((1, cw_in), lambda c: (0, c)), st, st, anyspec, anyspec],
        out_specs=[urow, urow, st, st],
        out_shape=[jax.ShapeDtypeStruct(z.shape, z.dtype), jax.ShapeDtypeStruct(zb.shape, zb.dtype),
                   jax.ShapeDtypeStruct(h0_re.shape, F32), jax.ShapeDtypeStruct(h0_im.shape, F32)],
        input_output_aliases={10: 0, 11: 1},
        compiler_params=_cp("parallel"),
        name="s5_step",
    )(u, bd_re, bd_im, pw_re, pw_im, cd_re, cd_im, d_skip.reshape(1, wd), h0_re, h0_im, z, zb)


def _lane_repeated(m):
    g, grp, n_state = m.shape
    assert LANES % n_state == 0
    return jnp.tile(m.reshape(g * grp, n_state), (1, LANES // n_state)).astype(BF16)


def kernel(x_prompt, x_sample, cache_ckv, cache_kpe, page_table, state_pool, state_ssm_re, state_ssm_im, norm_gains, w_ffn_gate, w_ffn_up, w_ffn_down, w_in0, w_pool, pool_scale, q_norm, kv_norm, w_uq, w_uk, w_uv, w_out0, w_in1, lam_re, lam_im, log_dt, b_re, b_im, c_re, c_im, d_skip, w_glu, w_out1):
    batch, seq, d_model = x_prompt.shape
    bd, dec_seq, _ = x_sample.shape
    assert dec_seq == 1, "the sample group carries one new token per sequence"
    n_prompt = batch * seq
    n_tok = n_prompt + bd
    pool_buf, pool_width = state_pool.shape[1], state_pool.shape[2]
    assert seq >= pool_buf and n_prompt % bd == 0
    kv_lora, heads, qk_nope = w_uk.shape
    v_head = w_uv.shape[2]
    q_lora = q_norm.shape[0]
    qk_rope = cache_kpe.shape[2]
    half = qk_rope // 2
    past_len = page_table.shape[1] * cache_ckv.shape[1]
    sm_scale = float(qk_nope + qk_rope) ** -0.5
    groups, n_state, grp = b_re.shape
    depth = norm_gains.shape[0]
    assert depth == 2

    wg, wu, wd = w_ffn_gate, w_ffn_up, w_ffn_down
    o_q, o_kv, o_pe = pool_width, pool_width + q_lora, pool_width + q_lora + kv_lora
    swap = jnp.concatenate([jnp.arange(half, qk_rope), jnp.arange(half)])
    w_kpe = w_in0[:, o_pe:].astype(BF16)
    w_kpe_sw = w_kpe[:, swap]
    w_uq3 = w_uq.astype(BF16).reshape(q_lora, heads, qk_nope + qk_rope).transpose(1, 0, 2)
    w_qn, w_qr = w_uq3[:, :, :qk_nope], w_uq3[:, :, qk_nope:]
    w_qr_sw = w_qr[:, :, swap]
    w_uk_t = w_uk.astype(BF16).transpose(1, 2, 0)
    w_uv_t = w_uv.astype(BF16).transpose(1, 0, 2)
    w_poolb = w_pool.astype(BF16)
    cache_kpe_t = cache_kpe.transpose(0, 2, 1)

    inv_freq = ROPE_THETA ** (-jnp.arange(half, dtype=F32) / half)
    pos = jnp.concatenate([jnp.tile(jnp.arange(seq), batch), jnp.full((bd,), past_len)]).astype(F32)
    ang = pos[:, None] * inv_freq[None, :]
    cos, sin = jnp.cos(ang), jnp.sin(ang)
    cosf = jnp.concatenate([cos, cos], axis=1)
    sinf = jnp.concatenate([-sin, sin], axis=1)

    g = norm_gains[0]
    x, h = _prenorm(x_prompt.reshape(n_prompt, d_model), x_sample.reshape(bd, d_model), g[0])
    x, h = _post(x, _ffn(h, wg, wu, wd, 0, 0), g[1], 0.5, g[2])

    u_pool, cq, ckv, ckv_b, kpe, kpe_b = _mla_prep(h, w_in0, o_q, o_kv, w_kpe, w_kpe_sw, q_norm, kv_norm,
                                                   cosf, sinf)
    y_pool_p = _pool_prompt(u_pool, w_poolb, pool_scale, batch, seq)
    ext_t = jnp.concatenate([state_pool.transpose(1, 0, 2), u_pool[None, n_prompt:]], axis=0)
    y_pool = _pool_sample(ext_t, w_poolb, pool_scale, past_len, y_pool_p, n_prompt)

    q_lat, q_pe = _q_prep(cq, w_qn, w_qr, w_qr_sw, w_uk_t, cosf, sinf)
    o = _attn_prompt(q_lat, q_pe, ckv_b, kpe_b, w_uv_t, batch, seq, sm_scale)
    o_lat_s = _attn_sample(q_lat[:, n_prompt:].transpose(1, 0, 2), q_pe[:, n_prompt:].transpose(1, 0, 2),
                           ckv[n_prompt:, None, :], kpe[n_prompt:, None, :],
                           cache_ckv, cache_kpe_t, page_table, sm_scale)
    o = _o_proj(o_lat_s.transpose(1, 0, 2), w_uv_t, o, n_prompt)
    y = _mm2(y_pool, o, w_out0)

    x, h = _post(x, y, g[3], 1.0, g[4])
    g1 = norm_gains[1]
    x, h = _post(x, _ffn(h, wg, wu, wd, 0, 1), g[5], 0.5, g1[0])

    x, h = _post(x, _ffn(h, wg, wu, wd, 1, 0), g1[1], 0.5, g1[2])

    u = _mm(h, w_in1)
    gpc = _tile(groups, (S5_CHUNK_GROUPS, 8, 4, 2, 1))
    n_chunks = groups // gpc
    cw_st = gpc * n_state
    bbt_re, bbt_im, pw_re, pw_im = _s5_disc(lam_re, lam_im, log_dt, b_re.transpose(2, 0, 1), b_im.transpose(2, 0, 1))
    bd_re = _lane_repeated(bbt_re.transpose(1, 0, 2))
    bd_im = _lane_repeated(bbt_im.transpose(1, 0, 2))
    cd_re = _lane_repeated(c_re)
    cd_im = _lane_repeated(c_im)
    pw_re = pw_re.reshape(n_chunks, 1, cw_st)
    pw_im = pw_im.reshape(n_chunks, 1, cw_st)
    zero_h = jnp.zeros((batch, n_chunks, 1, cw_st), F32)
    z, zb, hr_p, hi_p = _s5_scan(u, 0, batch, seq, grp, n_state, bd_re, bd_im, pw_re, pw_im, cd_re, cd_im,
                                 d_skip, zero_h, zero_h)
    z, zb, hr_s, hi_s = _s5_step(u, n_prompt, bd, grp, n_state, bd_re, bd_im, pw_re, pw_im, cd_re, cd_im,
                                 d_skip, state_ssm_re.reshape(bd, groups * n_state),
                                 state_ssm_im.reshape(bd, groups * n_state), z, zb)
    y = _mm(_mm_glu(zb, w_glu, z), w_out1)

    x, h = _post(x, y, g1[3], 1.0, g1[4])
    xp, xs = _post_last(x, _ffn(h, wg, wu, wd, 1, 1), g1[5], 0.5, n_prompt)

    y_prompt = xp.reshape(batch, seq, d_model)
    y_sample = xs.reshape(bd, 1, d_model)
    pool_p = u_pool[:n_prompt].reshape(batch, seq, pool_width)[:, seq - pool_buf:]
    pool_s = jnp.concatenate([state_pool[:, 1:], u_pool[n_prompt:, None, :]], axis=1)
    ckv_p = ckv[:n_prompt].reshape(batch, seq, kv_lora)
    ckv_s = ckv[n_prompt:].reshape(bd, 1, kv_lora)
    kpe_p = kpe[:n_prompt].reshape(batch, seq, qk_rope)
    kpe_s = kpe[n_prompt:].reshape(bd, 1, qk_rope)
    return (y_prompt, y_sample, pool_p, pool_s, ckv_p, ckv_s, kpe_p, kpe_s,
            hr_p.reshape(batch, groups, n_state), hi_p.reshape(batch, groups, n_state),
            hr_s.reshape(bd, groups, n_state), hi_s.reshape(bd, groups, n_state))
```

```python
import functools

import jax
import jax.numpy as jnp
from jax import lax
from jax.experimental import pallas as pl
from jax.experimental.pallas import tpu as pltpu

F32 = jnp.float32
BF16 = jnp.bfloat16

EPS = 1e-6
ROPE_THETA = 10000.0
POOL_WINDOWS = (2, 4, 8, 16)
NEG_INF = -1e30

V7X_VMEM_LIMIT_BYTES = 56 * 1024 * 1024
FFN_VMEM_LIMIT_BYTES = 58 * 1024 * 1024
LANES = 128
SUBLANES = 8
S5_CHUNK_GROUPS = 16
PAGE_RING_SLOTS = 3


def _tile(n, prefs):
    for p in prefs:
        if n % p == 0:
            return p
    return n


def _cp(*sem):
    return pltpu.CompilerParams(dimension_semantics=sem, vmem_limit_bytes=V7X_VMEM_LIMIT_BYTES)


def _rms(x, g):
    return x * lax.rsqrt(jnp.mean(x * x, axis=-1, keepdims=True) + EPS) * g


def _dot(a, b):
    return jnp.dot(a, b, preferred_element_type=F32)


def _dot_nt(a, b):
    return lax.dot_general(a, b, (((1,), (1,)), ((), ())), preferred_element_type=F32)


def _rep(x, n):
    return x if n == 1 else jnp.concatenate([x] * n, axis=1)


def _prenorm_kernel(xp_ref, xs_ref, g_ref, x_ref, h_ref, *, n_prompt_tiles):
    i = pl.program_id(0)

    def emit(src_ref):
        x = src_ref[...]
        x_ref[...] = x
        h_ref[...] = _rms(x, g_ref[...]).astype(h_ref.dtype)

    @pl.when(i < n_prompt_tiles)
    def _():
        emit(xp_ref)

    @pl.when(i >= n_prompt_tiles)
    def _():
        emit(xs_ref)


def _prenorm(x_prompt, x_sample, g):
    n_prompt, d = x_prompt.shape
    tm = x_sample.shape[0]
    assert n_prompt % tm == 0 and tm % (2 * SUBLANES) == 0
    npt = n_prompt // tm
    t = n_prompt + tm
    row = pl.BlockSpec((tm, d), lambda i: (i, 0))
    return pl.pallas_call(
        functools.partial(_prenorm_kernel, n_prompt_tiles=npt),
        grid=(npt + 1,),
        in_specs=[pl.BlockSpec((tm, d), lambda i: (jnp.minimum(i, npt - 1), 0)),
                  pl.BlockSpec((tm, d), lambda i: (0, 0)), pl.BlockSpec((1, d), lambda i: (0, 0))],
        out_specs=[row, row],
        out_shape=[jax.ShapeDtypeStruct((t, d), F32), jax.ShapeDtypeStruct((t, d), BF16)],
        compiler_params=_cp("parallel"),
        name="prenorm",
    )(x_prompt, x_sample, g.reshape(1, d))


def _post_kernel(x_ref, y_ref, gp_ref, gn_ref, xo_ref, ho_ref, *, scale):
    xn = x_ref[...] + scale * _rms(y_ref[...], gp_ref[...])
    xo_ref[...] = xn
    ho_ref[...] = _rms(xn, gn_ref[...]).astype(ho_ref.dtype)


def _post_last_kernel(x_ref, y_ref, gp_ref, xp_ref, xs_ref, *, scale, n_prompt_tiles):
    i = pl.program_id(0)
    xn = x_ref[...] + scale * _rms(y_ref[...], gp_ref[...])

    @pl.when(i < n_prompt_tiles)
    def _():
        xp_ref[...] = xn

    @pl.when(i >= n_prompt_tiles)
    def _():
        xs_ref[...] = xn


def _post_last(x, y, g_post, scale, n_prompt):
    t, d = x.shape
    tm = t - n_prompt
    assert n_prompt % tm == 0 and tm % SUBLANES == 0
    npt = n_prompt // tm
    row = pl.BlockSpec((tm, d), lambda i: (i, 0))
    vec = pl.BlockSpec((1, d), lambda i: (0, 0))
    return pl.pallas_call(
        functools.partial(_post_last_kernel, scale=scale, n_prompt_tiles=npt),
        grid=(t // tm,),
        in_specs=[row, row, vec],
        out_specs=[pl.BlockSpec((tm, d), lambda i: (jnp.minimum(i, npt - 1), 0)),
                   pl.BlockSpec((tm, d), lambda i: (0, 0))],
        out_shape=[jax.ShapeDtypeStruct((n_prompt, d), F32), jax.ShapeDtypeStruct((tm, d), F32)],
        compiler_params=_cp("arbitrary"),
        name="post_last",
    )(x, y, g_post.reshape(1, d))


def _post(x, y, g_post, scale, g_next):
    t, d = x.shape
    tm = _tile(t, (208, 128, 64, 32, 16))
    row = pl.BlockSpec((tm, d), lambda i: (i, 0))
    vec = pl.BlockSpec((1, d), lambda i: (0, 0))
    return pl.pallas_call(
        functools.partial(_post_kernel, scale=scale),
        grid=(t // tm,),
        in_specs=[row, row, vec, vec],
        out_specs=[row, row],
        out_shape=[jax.ShapeDtypeStruct((t, d), F32), jax.ShapeDtypeStruct((t, d), BF16)],
        compiler_params=_cp("parallel"),
        name="post",
    )(x, y, g_post.reshape(1, d), g_next.reshape(1, d))


def _ffn_kernel(h_ref, wg_ref, wu_ref, wd_ref, o_ref, *, n_chunks):
    j = pl.program_id(1)

    @pl.when(j == 0)
    def _():
        o_ref[...] = jnp.zeros(o_ref.shape, o_ref.dtype)

    h = h_ref[...]
    g = _dot(h, wg_ref[...].astype(BF16))
    u = _dot(h, wu_ref[...].astype(BF16))
    a = (g * jax.nn.sigmoid(g) * u).astype(BF16)
    cw = o_ref.shape[1] // n_chunks
    for c in range(n_chunks):
        o_ref[:, c * cw:(c + 1) * cw] += _dot(a, wd_ref[:, c * cw:(c + 1) * cw].astype(BF16))


def _ffn(h, wg, wu, wd, layer, half):
    t, d = h.shape
    f = wg.shape[-1]
    tm = _tile(t, (1040, 512, 256, 128, 64, 32, 16))
    tf = _tile(f, (256, 128))
    n_chunks = max(1, d // 512)
    return pl.pallas_call(
        functools.partial(_ffn_kernel, n_chunks=n_chunks),
        grid=(t // tm, f // tf),
        in_specs=[
            pl.BlockSpec((tm, d), lambda i, j: (i, 0), pipeline_mode=pl.Buffered(1)),
            pl.BlockSpec((None, None, d, tf), lambda i, j: (layer, half, 0, j)),
            pl.BlockSpec((None, None, d, tf), lambda i, j: (layer, half, 0, j)),
            pl.BlockSpec((None, None, tf, d), lambda i, j: (layer, half, j, 0)),
        ],
        out_specs=pl.BlockSpec((tm, d), lambda i, j: (i, 0), pipeline_mode=pl.Buffered(1)),
        out_shape=jax.ShapeDtypeStruct((t, d), F32),
        compiler_params=pltpu.CompilerParams(dimension_semantics=("parallel", "arbitrary"),
                                             vmem_limit_bytes=FFN_VMEM_LIMIT_BYTES),
        name="ffn",
    )(h, wg, wu, wd)


def _mm_kernel(x_ref, w_ref, o_ref):
    o_ref[...] = _dot(x_ref[...], w_ref[...].astype(BF16)).astype(o_ref.dtype)


def _mm_specs(t, k, n):
    tm = _tile(t, (1040, 512, 256, 128, 64, 32, 16))
    tn = _tile(n, (512, 256, 128))
    return tm, tn


def _mm(x, w):
    t, k = x.shape
    n = w.shape[1]
    tm, tn = _mm_specs(t, k, n)
    return pl.pallas_call(
        _mm_kernel,
        grid=(t // tm, n // tn),
        in_specs=[pl.BlockSpec((tm, k), lambda i, j: (i, 0)), pl.BlockSpec((k, tn), lambda i, j: (0, j))],
        out_specs=pl.BlockSpec((tm, tn), lambda i, j: (i, j)),
        out_shape=jax.ShapeDtypeStruct((t, n), F32),
        compiler_params=_cp("parallel", "arbitrary"),
        name="mm",
    )(x, w)


def _mm2_kernel(x1_ref, x2_ref, w_ref, o_ref):
    k1 = x1_ref.shape[1]
    o_ref[...] = (_dot(x1_ref[...], w_ref[:k1, :].astype(BF16))
                  + _dot(x2_ref[...], w_ref[k1:, :].astype(BF16)))


def _mm2(x1, x2, w):
    t, k1 = x1.shape
    k2 = x2.shape[1]
    n = w.shape[1]
    tm, tn = _mm_specs(t, k1 + k2, n)
    return pl.pallas_call(
        _mm2_kernel,
        grid=(t // tm, n // tn),
        in_specs=[pl.BlockSpec((tm, k1), lambda i, j: (i, 0)), pl.BlockSpec((tm, k2), lambda i, j: (i, 0)),
                  pl.BlockSpec((k1 + k2, tn), lambda i, j: (0, j))],
        out_specs=pl.BlockSpec((tm, tn), lambda i, j: (i, j)),
        out_shape=jax.ShapeDtypeStruct((t, n), F32),
        compiler_params=_cp("parallel", "arbitrary"),
        name="mm2",
    )(x1, x2, w)


def _mm_glu_kernel(zb_ref, w_ref, z_ref, o_ref):
    s = _dot(zb_ref[...], w_ref[...].astype(BF16))
    o_ref[...] = (z_ref[...] * jax.nn.sigmoid(s)).astype(o_ref.dtype)


def _mm_glu(zb, w, z):
    t, k = zb.shape
    n = w.shape[1]
    tm, tn = _mm_specs(t, k, n)
    return pl.pallas_call(
        _mm_glu_kernel,
        grid=(t // tm, n // tn),
        in_specs=[pl.BlockSpec((tm, k), lambda i, j: (i, 0)), pl.BlockSpec((k, tn), lambda i, j: (0, j)),
                  pl.BlockSpec((tm, tn), lambda i, j: (i, j))],
        out_specs=pl.BlockSpec((tm, tn), lambda i, j: (i, j)),
        out_shape=jax.ShapeDtypeStruct((t, n), BF16),
        compiler_params=_cp("parallel", "arbitrary"),
        name="mm_glu",
    )(zb, w, z)


def _pool_prompt_kernel(u_ref, wp_ref, ps_ref, o_ref):
    s_len = u_ref.shape[0]
    pg = wp_ref.shape[1]
    row = lax.broadcasted_iota(jnp.int32, (s_len, pg), 0)
    for gi, w in enumerate(POOL_WINDOWS):
        cols = slice(gi * pg, (gi + 1) * pg)
        u = u_ref[:, cols]
        s = u
        d = 1
        while d < w:
            s = s + jnp.where(row >= d, pltpu.roll(s, d, axis=0), 0.0)
            d *= 2
        cnt = jnp.minimum(row + 1, w).astype(F32)
        dd = (s / cnt - u).astype(BF16)
        o_ref[:, cols] = (_dot(dd, wp_ref[gi]) * ps_ref[:, cols]).astype(o_ref.dtype)


def _pool_prompt(u_pool, w_pool, pool_scale, batch, seq):
    c = u_pool.shape[1]
    return pl.pallas_call(
        _pool_prompt_kernel,
        grid=(batch,),
        in_specs=[pl.BlockSpec((seq, c), lambda b: (b, 0)),
                  pl.BlockSpec(w_pool.shape, lambda b: (0, 0, 0)),
                  pl.BlockSpec((1, c), lambda b: (0, 0))],
        out_specs=pl.BlockSpec((seq, c), lambda b: (b, 0)),
        out_shape=jax.ShapeDtypeStruct((u_pool.shape[0], c), BF16),
        compiler_params=_cp("parallel"),
        name="pool_prompt",
    )(u_pool, w_pool, pool_scale.reshape(1, c))


def _pool_sample_kernel(ext_ref, wp_ref, ps_ref, o_in_ref, o_ref, *, start):
    del o_in_ref
    nb = ext_ref.shape[0]
    pg = wp_ref.shape[1]
    for gi, w in enumerate(POOL_WINDOWS):
        cols = slice(gi * pg, (gi + 1) * pg)
        u = ext_ref[nb - 1, :, cols]
        s = u
        for i in range(1, w):
            s = s + ext_ref[nb - 1 - i, :, cols]
        dd = (s / float(min(start + 1, w)) - u).astype(BF16)
        o_ref[:, cols] = (_dot(dd, wp_ref[gi]) * ps_ref[:, cols]).astype(o_ref.dtype)


def _pool_sample(ext_t, w_pool, pool_scale, start, y_full, row_off):
    nb, bd, c = ext_t.shape
    assert row_off % bd == 0
    rb = row_off // bd
    return pl.pallas_call(
        functools.partial(_pool_sample_kernel, start=start),
        grid=(1,),
        in_specs=[pl.BlockSpec((nb, bd, c), lambda i: (0, 0, 0)),
                  pl.BlockSpec(w_pool.shape, lambda i: (0, 0, 0)),
                  pl.BlockSpec((1, c), lambda i: (0, 0)),
                  pl.BlockSpec(memory_space=pl.ANY)],
        out_specs=pl.BlockSpec((bd, c), lambda i: (rb, 0)),
        out_shape=jax.ShapeDtypeStruct(y_full.shape, y_full.dtype),
        input_output_aliases={3: 0},
        compiler_params=_cp("arbitrary"),
        name="pool_sample",
    )(ext_t, w_pool, pool_scale.reshape(1, c), y_full)


def _mla_prep_kernel(h_ref, wp_ref, wq_ref, wkv_ref, wkp_ref, wkps_ref, qn_ref, kvn_ref, cos_ref, sin_ref,
                     up_ref, cq_ref, ckv_ref, ckvb_ref, kpe_ref, kpeb_ref):
    h = h_ref[...]
    up_ref[...] = _dot(h, wp_ref[...].astype(BF16))
    cq_ref[...] = _rms(_dot(h, wq_ref[...].astype(BF16)), qn_ref[...]).astype(cq_ref.dtype)
    ckv = _rms(_dot(h, wkv_ref[...].astype(BF16)), kvn_ref[...])
    ckv_ref[...] = ckv
    ckvb_ref[...] = ckv.astype(ckvb_ref.dtype)
    kpe = _dot(h, wkp_ref[...]) * cos_ref[...] + _dot(h, wkps_ref[...]) * sin_ref[...]
    kpe_ref[...] = kpe
    kpeb_ref[...] = kpe.astype(kpeb_ref.dtype)


def _mla_prep(h, w_in0, o_q, o_kv, w_kpe, w_kpe_sw, q_norm, kv_norm, cosf, sinf):
    t, d = h.shape
    ql, kl, r = q_norm.shape[0], kv_norm.shape[0], w_kpe.shape[1]
    pw = o_q
    assert o_q % ql == 0 and o_kv % kl == 0
    tm = _tile(t, (208, 128, 64, 32, 16))
    row = lambda n: pl.BlockSpec((tm, n), lambda i: (i, 0))
    full = lambda a, b: pl.BlockSpec((a, b), lambda i: (0, 0))
    win = lambda n, off: pl.BlockSpec((d, n), lambda i: (0, off // n), pipeline_mode=pl.Buffered(1))
    return pl.pallas_call(
        _mla_prep_kernel,
        grid=(t // tm,),
        in_specs=[row(d), win(pw, 0), win(ql, o_q), win(kl, o_kv), full(d, r), full(d, r), full(1, ql), full(1, kl),
                  row(r), row(r)],
        out_specs=[row(pw), row(ql), row(kl), row(kl), row(r), row(r)],
        out_shape=[jax.ShapeDtypeStruct((t, pw), F32), jax.ShapeDtypeStruct((t, ql), BF16),
                   jax.ShapeDtypeStruct((t, kl), F32), jax.ShapeDtypeStruct((t, kl), BF16),
                   jax.ShapeDtypeStruct((t, r), F32), jax.ShapeDtypeStruct((t, r), BF16)],
        compiler_params=_cp("parallel"),
        name="mla_prep",
    )(h, w_in0, w_in0, w_in0, w_kpe, w_kpe_sw, q_norm.reshape(1, ql), kv_norm.reshape(1, kl), cosf, sinf)


def _q_prep_kernel(cq_ref, wqn_ref, wqr_ref, wqrs_ref, wuk_ref, cos_ref, sin_ref, ql_ref, qp_ref):
    cq = cq_ref[...]
    hp, nope, _ = wuk_ref.shape
    r = qp_ref.shape[2]
    q_nope = _dot(cq, wqn_ref[0]).astype(BF16)
    q_pe = _dot(cq, wqr_ref[0]) * cos_ref[...] + _dot(cq, wqrs_ref[0]) * sin_ref[...]
    for hh in range(hp):
        ql_ref[hh] = _dot(q_nope[:, hh * nope:(hh + 1) * nope], wuk_ref[hh]).astype(ql_ref.dtype)
        qp_ref[hh] = q_pe[:, hh * r:(hh + 1) * r].astype(qp_ref.dtype)


def _group_heads(w, hp):
    heads, a, b = w.shape
    return w.reshape(heads // hp, hp, a, b).transpose(0, 2, 1, 3).reshape(heads // hp, a, hp * b)


def _q_prep(cq, w_qn, w_qr, w_qr_sw, w_uk_t, cosf, sinf):
    t, ql = cq.shape
    heads, _, nope = w_qn.shape
    r = w_qr.shape[2]
    kl = w_uk_t.shape[2]
    hp = _tile(heads, (4, 2, 1))
    tm = _tile(t, (1040, 512, 256, 128, 64, 32, 16))
    hw = lambda a, b: pl.BlockSpec((1, a, b), lambda i, h: (h, 0, 0))
    rows = lambda n: pl.BlockSpec((tm, n), lambda i, h: (i, 0))
    return pl.pallas_call(
        _q_prep_kernel,
        grid=(t // tm, heads // hp),
        in_specs=[rows(ql), hw(ql, hp * nope), hw(ql, hp * r), hw(ql, hp * r),
                  pl.BlockSpec((hp, nope, kl), lambda i, h: (h, 0, 0)), rows(hp * r), rows(hp * r)],
        out_specs=[pl.BlockSpec((hp, tm, kl), lambda i, h: (h, i, 0)),
                   pl.BlockSpec((hp, tm, r), lambda i, h: (h, i, 0))],
        out_shape=[jax.ShapeDtypeStruct((heads, t, kl), BF16), jax.ShapeDtypeStruct((heads, t, r), BF16)],
        compiler_params=_cp("parallel", "arbitrary"),
        name="q_prep",
    )(cq, _group_heads(w_qn, hp), _group_heads(w_qr, hp), _group_heads(w_qr_sw, hp), w_uk_t,
      jnp.tile(cosf, (1, hp)), jnp.tile(sinf, (1, hp)))


def _o_proj_kernel(ol_ref, wuv_ref, o_in_ref, o_ref):
    del o_in_ref
    o_ref[...] = _dot(ol_ref[0], wuv_ref[0]).astype(o_ref.dtype)


def _o_proj(o_lat, w_uv_t, o_full, row_off):
    heads, n, kl = o_lat.shape
    vh = w_uv_t.shape[2]
    assert row_off % n == 0
    rb = row_off // n
    return pl.pallas_call(
        _o_proj_kernel,
        grid=(heads,),
        in_specs=[pl.BlockSpec((1, n, kl), lambda h: (h, 0, 0)), pl.BlockSpec((1, kl, vh), lambda h: (h, 0, 0)),
                  pl.BlockSpec(memory_space=pl.ANY)],
        out_specs=pl.BlockSpec((n, vh), lambda h: (rb, h)),
        out_shape=jax.ShapeDtypeStruct(o_full.shape, o_full.dtype),
        input_output_aliases={2: 0},
        compiler_params=_cp("arbitrary"),
        name="o_proj",
    )(o_lat, w_uv_t, o_full)


def _softmax_update(s, v, m_ref, l_ref, acc_ref):
    m_prev = m_ref[...]
    m_new = jnp.maximum(m_prev, jnp.max(s, axis=1, keepdims=True))
    alpha = jnp.exp(m_prev - m_new)
    p = jnp.exp(s - _rep(m_new, s.shape[1] // LANES))
    l_ref[...] = alpha * l_ref[...] + jnp.sum(p, axis=1, keepdims=True)
    acc_ref[...] = acc_ref[...] * _rep(alpha, acc_ref.shape[1] // LANES) + _dot(p.astype(BF16), v)
    m_ref[...] = m_new


def _attn_prompt_kernel(ql_ref, qp_ref, k_ref, kp_ref, wuv_ref, o_ref, m_ref, l_ref, acc_ref, *, scale):
    qi = pl.program_id(1)
    hg, tq, c = ql_ref.shape
    r = hg * tq
    q = ql_ref[...].reshape(r, c)
    qp = qp_ref[...].reshape(r, qp_ref.shape[2])
    m_ref[...] = jnp.full(m_ref.shape, NEG_INF, F32)
    l_ref[...] = jnp.zeros(l_ref.shape, F32)
    acc_ref[...] = jnp.zeros(acc_ref.shape, F32)

    def block(start, width, masked):
        k = k_ref[pl.ds(start, width), :]
        kp = kp_ref[pl.ds(start, width), :]
        s = (_dot_nt(q, k) + _dot_nt(qp, kp)) * scale
        if masked:
            s3 = s.reshape(hg, tq, width)
            qpos = qi * tq + lax.broadcasted_iota(jnp.int32, s3.shape, 1)
            kpos = start + lax.broadcasted_iota(jnp.int32, s3.shape, 2)
            s = jnp.where(kpos <= qpos, s3, NEG_INF).reshape(r, width)
        _softmax_update(s, k, m_ref, l_ref, acc_ref)

    def body(kb, carry):
        block(pl.multiple_of(kb * (2 * tq), 2 * tq), 2 * tq, False)
        return carry

    lax.fori_loop(0, qi // 2, body, 0)

    @pl.when(qi % 2 == 0)
    def _():
        block(pl.multiple_of(qi * tq, tq), tq, True)

    @pl.when(qi % 2 == 1)
    def _():
        block(pl.multiple_of((qi - 1) * tq, 2 * tq), 2 * tq, True)

    o = (acc_ref[...] / _rep(l_ref[...], c // LANES)).astype(BF16)
    vh = wuv_ref.shape[2]
    for hh in range(hg):
        o_ref[:, hh * vh:(hh + 1) * vh] = _dot(o[hh * tq:(hh + 1) * tq, :], wuv_ref[hh]).astype(o_ref.dtype)


def _attn_prompt(q_lat, q_pe, ckv_b, kpe_b, w_uv_t, batch, seq, scale):
    heads, n_tok, c = q_lat.shape
    r = q_pe.shape[2]
    vh = w_uv_t.shape[2]
    tq = _tile(seq, (256, 128))
    hg = _tile(heads, (12, 8, 4, 2, 1))
    nq = seq // tq
    rows = hg * tq
    return pl.pallas_call(
        functools.partial(_attn_prompt_kernel, scale=scale),
        grid=(batch, nq, heads // hg),
        in_specs=[pl.BlockSpec((hg, tq, c), lambda b, i, g: (g, b * nq + i, 0)),
                  pl.BlockSpec((hg, tq, r), lambda b, i, g: (g, b * nq + i, 0)),
                  pl.BlockSpec((seq, c), lambda b, i, g: (b, 0)),
                  pl.BlockSpec((seq, r), lambda b, i, g: (b, 0)),
                  pl.BlockSpec((hg, c, vh), lambda b, i, g: (g, 0, 0))],
        out_specs=pl.BlockSpec((tq, hg * vh), lambda b, i, g: (b * nq + i, g)),
        out_shape=jax.ShapeDtypeStruct((n_tok, heads * vh), BF16),
        scratch_shapes=[pltpu.VMEM((rows, LANES), F32), pltpu.VMEM((rows, LANES), F32),
                        pltpu.VMEM((rows, c), F32)],
        compiler_params=_cp("parallel", "parallel", "arbitrary"),
        name="attn_prompt",
    )(q_lat, q_pe, ckv_b, kpe_b, w_uv_t)


def _attn_sample_kernel(pt_ref, ql_ref, qp_ref, cn_ref, pn_ref, ck_hbm, kp_hbm, o_ref,
                        kbuf, pbuf, sem, m_ref, l_ref, acc_ref, *, pps, scale):
    n_slots = kbuf.shape[0]
    ahead = n_slots - 1
    n_steps = pl.num_programs(1)
    step = pl.program_id(1)
    gstep = pl.program_id(0) * n_steps + step
    total = pl.num_programs(0) * n_steps
    slot = gstep % n_slots

    def page_copies(g, slot_idx, i):
        pid = pt_ref[g * pps + i]
        return (pltpu.make_async_copy(ck_hbm.at[pid], kbuf.at[slot_idx, i], sem.at[0, slot_idx]),
                pltpu.make_async_copy(kp_hbm.at[pid], pbuf.at[slot_idx, i], sem.at[1, slot_idx]))

    def start_fetch(g, slot_idx):
        for i in range(pps):
            for cp in page_copies(g, slot_idx, i):
                cp.start()

    @pl.when(gstep == 0)
    def _():
        for g in range(ahead):
            start_fetch(g, g)

    @pl.when(gstep + ahead < total)
    def _():
        start_fetch(gstep + ahead, (gstep + ahead) % n_slots)

    for i in range(pps):
        for cp in page_copies(gstep, slot, i):
            cp.wait()

    q = ql_ref[0]
    qp = qp_ref[0]

    @pl.when(step == 0)
    def _():
        cn = cn_ref[0]
        pn = pn_ref[0]
        s0 = (jnp.sum(q.astype(F32) * cn, axis=1, keepdims=True)
              + jnp.sum(qp.astype(F32) * pn, axis=1, keepdims=True)) * scale
        m_ref[...] = jnp.broadcast_to(s0, m_ref.shape)
        l_ref[...] = jnp.ones(l_ref.shape, F32)
        acc_ref[...] = jnp.broadcast_to(cn, acc_ref.shape)

    page, c = kbuf.shape[2], kbuf.shape[3]
    k = kbuf[slot].reshape(pps * page, c).astype(BF16)
    kp_t = jnp.concatenate([pbuf[slot, i].astype(BF16) for i in range(pps)], axis=1)
    s = (_dot_nt(q, k) + _dot(qp, kp_t)) * scale
    _softmax_update(s, k, m_ref, l_ref, acc_ref)

    @pl.when(step == pl.num_programs(1) - 1)
    def _():
        o_ref[0] = (acc_ref[...] / _rep(l_ref[...], acc_ref.shape[1] // LANES)).astype(o_ref.dtype)


def _attn_sample(q_lat, q_pe, ckv_new, kpe_new, cache_ckv, cache_kpe_t, page_table, scale):
    bd, heads, c = q_lat.shape
    r = q_pe.shape[2]
    n_pages = page_table.shape[1]
    page = cache_ckv.shape[1]
    pps = _tile(n_pages, (32, 16, 8, 4, 2, 1))
    assert bd * (n_pages // pps) >= PAGE_RING_SLOTS - 1

    per_seq = lambda n, w: pl.BlockSpec((1, n, w), lambda b, s, pt: (b, 0, 0))
    hbm = pl.BlockSpec(memory_space=pl.ANY)
    return pl.pallas_call(
        functools.partial(_attn_sample_kernel, pps=pps, scale=scale),
        grid_spec=pltpu.PrefetchScalarGridSpec(
            num_scalar_prefetch=1,
            grid=(bd, n_pages // pps),
            in_specs=[per_seq(heads, c), per_seq(heads, r), per_seq(1, c), per_seq(1, r), hbm, hbm],
            out_specs=per_seq(heads, c),
            scratch_shapes=[pltpu.VMEM((PAGE_RING_SLOTS, pps, page, c), F32),
                            pltpu.VMEM((PAGE_RING_SLOTS, pps, r, page), F32),
                            pltpu.SemaphoreType.DMA((2, PAGE_RING_SLOTS)),
                            pltpu.VMEM((heads, LANES), F32), pltpu.VMEM((heads, LANES), F32),
                            pltpu.VMEM((heads, c), F32)],
        ),
        out_shape=jax.ShapeDtypeStruct((bd, heads, c), BF16),
        compiler_params=_cp("arbitrary", "arbitrary"),
        name="attn_sample",
    )(page_table.reshape(-1), q_lat, q_pe, ckv_new, kpe_new, cache_ckv, cache_kpe_t)


def _s5_disc_kernel(lr_ref, li_ref, ldt_ref, br_ref, bi_ref, bbr_ref, bbi_ref, ar_ref, ai_ref):
    dt = jnp.exp(ldt_ref[...])
    lr = lr_ref[...]
    li = li_ref[...]
    mag = jnp.exp(lr * dt)
    a_re = mag * jnp.cos(li * dt)
    a_im = mag * jnp.sin(li * dt)
    den = lr * lr + li * li
    nr, ni = a_re - 1.0, a_im
    f_re = (nr * lr + ni * li) / den
    f_im = (ni * lr - nr * li) / den
    for k in range(br_ref.shape[0]):
        bbr_ref[k] = f_re * br_ref[k] - f_im * bi_ref[k]
        bbi_ref[k] = f_re * bi_ref[k] + f_im * br_ref[k]
    ar_ref[...] = a_re
    ai_ref[...] = a_im


def _s5_disc(lam_re, lam_im, log_dt, b_re_t, b_im_t):
    k, g, n = b_re_t.shape
    spec2 = pl.BlockSpec((g, n), lambda i: (0, 0))
    spec3 = lambda a: pl.BlockSpec((a, g, n), lambda i: (0, 0, 0))
    return pl.pallas_call(
        _s5_disc_kernel,
        grid=(1,),
        in_specs=[spec2, spec2, spec2, spec3(k), spec3(k)],
        out_specs=[spec3(k), spec3(k), spec2, spec2],
        out_shape=[jax.ShapeDtypeStruct((k, g, n), F32)] * 2 + [jax.ShapeDtypeStruct((g, n), F32)] * 2,
        compiler_params=_cp("arbitrary"),
        name="s5_disc",
    )(lam_re, lam_im, jnp.broadcast_to(log_dt[:, None], (g, n)), b_re_t, b_im_t)


def _idiv(x, d):
    return x >> (d.bit_length() - 1) if d & (d - 1) == 0 else x // d


def _expand_block_diag(dd, grp, n_state, cw_st):
    tiled = _rep(dd, cw_st // LANES)
    r = lax.broadcasted_iota(jnp.int32, tiled.shape, 0)
    c = lax.broadcasted_iota(jnp.int32, tiled.shape, 1)
    return jnp.where(_idiv(r, grp) == _idiv(c, n_state), tiled, jnp.zeros_like(tiled))


def _s5_scan_kernel(u_ref, bdr_ref, bdi_ref, pr_ref, pi_ref, cdr_ref, cdi_ref, d_ref, h0r_ref, h0i_ref,
                    z_ref, zb_ref, hr_out_ref, hi_out_ref,
                    bbr_ref, bbi_ref, cmr_ref, cmi_ref,
                    st_ref, up_ref, sr_ref, si_ref, ir_ref, ii_ref, cr_carry, ci_carry, *, grp, n_state):
    tc = pl.program_id(2)
    tt, w = sr_ref.shape
    nl = tt // SUBLANES
    assert nl & (nl - 1) == 0, "sub-block length must be a power of two (A_bar^nl by squaring)"

    @pl.when(tc == 0)
    def _():
        cr_carry[...] = h0r_ref[0, 0]
        ci_carry[...] = h0i_ref[0, 0]
        bbr_ref[...] = _expand_block_diag(bdr_ref[...], grp, n_state, w)
        bbi_ref[...] = _expand_block_diag(bdi_ref[...], grp, n_state, w)
        cmr_ref[...] = _expand_block_diag(cdr_ref[...], grp, n_state, w)
        cmi_ref[...] = _expand_block_diag(cdi_ref[...], grp, n_state, w)

    n_lt = u_ref.shape[1] // LANES
    pitch = st_ref.shape[1] // SUBLANES
    for s in range(SUBLANES):
        for j in range(n_lt):
            st_ref[j, s * pitch:s * pitch + nl, :] = u_ref[s * nl:(s + 1) * nl, j * LANES:(j + 1) * LANES]
    for i in range(nl):
        for j in range(n_lt):
            up_ref[i * SUBLANES:(i + 1) * SUBLANES, j * LANES:(j + 1) * LANES] = (
                st_ref[j, pl.ds(i, SUBLANES, stride=pitch), :])
    ub = up_ref[...].astype(BF16)
    sr_ref[...] = _dot(ub, bbr_ref[...])
    si_ref[...] = _dot(ub, bbi_ref[...])
    a_re = pr_ref[0]
    a_im = pi_ref[0]
    a_re8 = jnp.broadcast_to(a_re, (SUBLANES, w))
    a_im8 = jnp.broadcast_to(a_im, (SUBLANES, w))

    def pass1(i, carry):
        h_re, h_im = carry
        off = pl.multiple_of(i * SUBLANES, SUBLANES)
        x_re = sr_ref[pl.ds(off, SUBLANES), :]
        x_im = si_ref[pl.ds(off, SUBLANES), :]
        h_re, h_im = a_re8 * h_re - a_im8 * h_im + x_re, a_re8 * h_im + a_im8 * h_re + x_im
        sr_ref[pl.ds(off, SUBLANES), :] = h_re
        si_ref[pl.ds(off, SUBLANES), :] = h_im
        return h_re, h_im

    zero = jnp.zeros((SUBLANES, w), F32)
    f_re, f_im = lax.fori_loop(0, nl, pass1, (zero, zero))

    al_re, al_im = a_re, a_im
    n = nl
    while n > 1:
        al_re, al_im = al_re * al_re - al_im * al_im, 2.0 * (al_re * al_im)
        n //= 2
    c_re, c_im = cr_carry[...], ci_carry[...]
    for s in range(SUBLANES):
        ir_ref[s:s + 1, :] = c_re
        ii_ref[s:s + 1, :] = c_im
        c_re, c_im = (f_re[s:s + 1, :] + al_re * c_re - al_im * c_im,
                      f_im[s:s + 1, :] + al_re * c_im + al_im * c_re)
    cr_carry[...] = c_re
    ci_carry[...] = c_im
    i_re = ir_ref[...]
    i_im = ii_ref[...]

    def pass2(i, carry):
        p_re, p_im = carry
        off = pl.multiple_of(i * SUBLANES, SUBLANES)
        p_re8 = jnp.broadcast_to(p_re, (SUBLANES, w))
        p_im8 = jnp.broadcast_to(p_im, (SUBLANES, w))
        sr_ref[pl.ds(off, SUBLANES), :] = sr_ref[pl.ds(off, SUBLANES), :] + (p_re8 * i_re - p_im8 * i_im)
        si_ref[pl.ds(off, SUBLANES), :] = si_ref[pl.ds(off, SUBLANES), :] + (p_re8 * i_im + p_im8 * i_re)
        return p_re * a_re - p_im * a_im, p_re * a_im + p_im * a_re

    lax.fori_loop(0, nl, pass2, (a_re, a_im))

    y = _dot_nt(sr_ref[...].astype(BF16), cmr_ref[...]) - _dot_nt(si_ref[...].astype(BF16), cmi_ref[...])
    for i in range(nl):
        for j in range(n_lt):
            st_ref[j, pl.ds(i, SUBLANES, stride=pitch), :] = (
                y[i * SUBLANES:(i + 1) * SUBLANES, j * LANES:(j + 1) * LANES])
    for s in range(SUBLANES):
        rows = slice(s * nl, (s + 1) * nl)
        for j in range(n_lt):
            cols = slice(j * LANES, (j + 1) * LANES)
            z = jax.nn.gelu(st_ref[j, s * pitch:s * pitch + nl, :] + d_ref[:, cols] * u_ref[rows, cols],
                            approximate=True)
            z_ref[rows, cols] = z
            zb_ref[rows, cols] = z.astype(zb_ref.dtype)

    @pl.when(tc == pl.num_programs(2) - 1)
    def _():
        hr_out_ref[0, 0] = c_re
        hi_out_ref[0, 0] = c_im


def _s5_scan(u, row_off, batch, seq, grp, n_state, bd_re, bd_im, pw_re, pw_im, cd_re, cd_im, d_skip, h0_re, h0_im):
    n_chunks, _, cw_st = pw_re.shape
    cw_in = cw_st // n_state * grp
    tt = _tile(seq, (2048, 1024, 512, 256, 128, 64, 32, 16, 8))
    assert seq % tt == 0 and row_off % tt == 0
    nt = seq // tt
    ob = row_off // tt
    cmat = lambda a, b: pl.BlockSpec((1, a, b), lambda b_, c, t: (c, 0, 0))
    dmat = pl.BlockSpec((cw_in, LANES), lambda b_, c, t: (c, 0))
    st = pl.BlockSpec((1, 1, 1, cw_st), lambda b_, c, t: (b_, c, 0, 0))
    urow = pl.BlockSpec((tt, cw_in), lambda b_, c, t: (ob + b_ * nt + t, c))
    orow = urow
    rows = u.shape[0]
    wd = n_chunks * cw_in
    return pl.pallas_call(
        functools.partial(_s5_scan_kernel, grp=grp, n_state=n_state),
        grid=(batch, n_chunks, nt),
        in_specs=[urow, dmat, dmat, cmat(1, cw_st), cmat(1, cw_st),
                  dmat, dmat, pl.BlockSpec((1, cw_in), lambda b_, c, t: (0, c)), st, st],
        out_specs=[orow, orow, st, st],
        out_shape=[jax.ShapeDtypeStruct((rows, wd), F32), jax.ShapeDtypeStruct((rows, wd), BF16),
                   jax.ShapeDtypeStruct(h0_re.shape, F32), jax.ShapeDtypeStruct(h0_im.shape, F32)],
        scratch_shapes=[pltpu.VMEM((cw_in, cw_st), BF16), pltpu.VMEM((cw_in, cw_st), BF16),
                        pltpu.VMEM((cw_in, cw_st), BF16), pltpu.VMEM((cw_in, cw_st), BF16),
                        pltpu.VMEM((cw_in // LANES, tt + SUBLANES * SUBLANES, LANES), F32),
                        pltpu.VMEM((tt, cw_in), F32),
                        pltpu.VMEM((tt, cw_st), F32), pltpu.VMEM((tt, cw_st), F32),
                        pltpu.VMEM((SUBLANES, cw_st), F32), pltpu.VMEM((SUBLANES, cw_st), F32),
                        pltpu.VMEM((1, cw_st), F32), pltpu.VMEM((1, cw_st), F32)],
        compiler_params=_cp("parallel", "parallel", "arbitrary"),
        name="s5_scan",
    )(u, bd_re, bd_im, pw_re, pw_im, cd_re, cd_im, d_skip.reshape(1, wd), h0_re, h0_im)


def _s5_step_kernel(u_ref, bdr_ref, bdi_ref, pr_ref, pi_ref, cdr_ref, cdi_ref, d_ref, h0r_ref, h0i_ref,
                    z_in_ref, zb_in_ref, z_ref, zb_ref, hr_out_ref, hi_out_ref, *, grp, n_state):
    del z_in_ref, zb_in_ref
    w = h0r_ref.shape[1]
    u = u_ref[...]
    ub = u.astype(BF16)
    a_re = pr_ref[0]
    a_im = pi_ref[0]
    h0r = h0r_ref[...]
    h0i = h0i_ref[...]
    h_re = _dot(ub, _expand_block_diag(bdr_ref[...], grp, n_state, w)) + (a_re * h0r - a_im * h0i)
    h_im = _dot(ub, _expand_block_diag(bdi_ref[...], grp, n_state, w)) + (a_re * h0i + a_im * h0r)
    hr_out_ref[...] = h_re
    hi_out_ref[...] = h_im
    y = (_dot_nt(h_re.astype(BF16), _expand_block_diag(cdr_ref[...], grp, n_state, w))
         - _dot_nt(h_im.astype(BF16), _expand_block_diag(cdi_ref[...], grp, n_state, w)) + d_ref[...] * u)
    z = jax.nn.gelu(y, approximate=True)
    z_ref[...] = z
    zb_ref[...] = z.astype(zb_ref.dtype)


def _s5_step(u, row_off, n_seq, grp, n_state, bd_re, bd_im, pw_re, pw_im, cd_re, cd_im, d_skip, h0_re, h0_im,
             z, zb):
    n_chunks, _, cw_st = pw_re.shape
    cw_in = cw_st // n_state * grp
    ob = row_off // n_seq
    cmat = lambda a, b: pl.BlockSpec((1, a, b), lambda c: (c, 0, 0))
    dmat = pl.BlockSpec((cw_in, LANES), lambda c: (c, 0))
    st = pl.BlockSpec((n_seq, cw_st), lambda c: (0, c))
    urow = pl.BlockSpec((n_seq, cw_in), lambda c: (ob, c))
    wd = n_chunks * cw_in
    anyspec = pl.BlockSpec(memory_space=pl.ANY)
    return pl.pallas_call(
        functools.partial(_s5_step_kernel, grp=grp, n_state=n_state),
        grid=(n_chunks,),
        in_specs=[urow, dmat, dmat, cmat(1, cw_st), cmat(1, cw_st), dmat, dmat,
                  pl.BlockSpec((1, cw_in), lambda c: (0, c)), st, st, anyspec, anyspec],
        out_specs=[urow, urow, st, st],
        out_shape=[jax.ShapeDtypeStruct(z.shape, z.dtype), jax.ShapeDtypeStruct(zb.shape, zb.dtype),
                   jax.ShapeDtypeStruct(h0_re.shape, F32), jax.ShapeDtypeStruct(h0_im.shape, F32)],
        input_output_aliases={10: 0, 11: 1},
        compiler_params=_cp("parallel"),
        name="s5_step",
    )(u, bd_re, bd_im, pw_re, pw_im, cd_re, cd_im, d_skip.reshape(1, wd), h0_re, h0_im, z, zb)


def _lane_repeated(m):
    g, grp, n_state = m.shape
    assert LANES % n_state == 0
    return jnp.tile(m.reshape(g * grp, n_state), (1, LANES // n_state)).astype(BF16)


def kernel(x_prompt, x_sample, cache_ckv, cache_kpe, page_table, state_pool, state_ssm_re, state_ssm_im, norm_gains, w_ffn_gate, w_ffn_up, w_ffn_down, w_in0, w_pool, pool_scale, q_norm, kv_norm, w_uq, w_uk, w_uv, w_out0, w_in1, lam_re, lam_im, log_dt, b_re, b_im, c_re, c_im, d_skip, w_glu, w_out1):
    batch, seq, d_model = x_prompt.shape
    bd, dec_seq, _ = x_sample.shape
    assert dec_seq == 1, "the sample group carries one new token per sequence"
    n_prompt = batch * seq
    n_tok = n_prompt + bd
    pool_buf, pool_width = state_pool.shape[1], state_pool.shape[2]
    assert seq >= pool_buf and n_prompt % bd == 0
    kv_lora, heads, qk_nope = w_uk.shape
    v_head = w_uv.shape[2]
    q_lora = q_norm.shape[0]
    qk_rope = cache_kpe.shape[2]
    half = qk_rope // 2
    past_len = page_table.shape[1] * cache_ckv.shape[1]
    sm_scale = float(qk_nope + qk_rope) ** -0.5
    groups, n_state, grp = b_re.shape
    depth = norm_gains.shape[0]
    assert depth == 2

    wg, wu, wd = w_ffn_gate, w_ffn_up, w_ffn_down
    o_q, o_kv, o_pe = pool_width, pool_width + q_lora, pool_width + q_lora + kv_lora
    swap = jnp.concatenate([jnp.arange(half, qk_rope), jnp.arange(half)])
    w_kpe = w_in0[:, o_pe:].astype(BF16)
    w_kpe_sw = w_kpe[:, swap]
    w_uq3 = w_uq.astype(BF16).reshape(q_lora, heads, qk_nope + qk_rope).transpose(1, 0, 2)
    w_qn, w_qr = w_uq3[:, :, :qk_nope], w_uq3[:, :, qk_nope:]
    w_qr_sw = w_qr[:, :, swap]
    w_uk_t = w_uk.astype(BF16).transpose(1, 2, 0)
    w_uv_t = w_uv.astype(BF16).transpose(1, 0, 2)
    w_poolb = w_pool.astype(BF16)
    cache_kpe_t = cache_kpe.transpose(0, 2, 1)

    inv_freq = ROPE_THETA ** (-jnp.arange(half, dtype=F32) / half)
    pos = jnp.concatenate([jnp.tile(jnp.arange(seq), batch), jnp.full((bd,), past_len)]).astype(F32)
    ang = pos[:, None] * inv_freq[None, :]
    cos, sin = jnp.cos(ang), jnp.sin(ang)
    cosf = jnp.concatenate([cos, cos], axis=1)
    sinf = jnp.concatenate([-sin, sin], axis=1)

    g = norm_gains[0]
    x, h = _prenorm(x_prompt.reshape(n_prompt, d_model), x_sample.reshape(bd, d_model), g[0])
    x, h = _post(x, _ffn(h, wg, wu, wd, 0, 0), g[1], 0.5, g[2])

    u_pool, cq, ckv, ckv_b, kpe, kpe_b = _mla_prep(h, w_in0, o_q, o_kv, w_kpe, w_kpe_sw, q_norm, kv_norm,
                                                   cosf, sinf)
    y_pool_p = _pool_prompt(u_pool, w_poolb, pool_scale, batch, seq)
    ext_t = jnp.concatenate([state_pool.transpose(1, 0, 2), u_pool[None, n_prompt:]], axis=0)
    y_pool = _pool_sample(ext_t, w_poolb, pool_scale, past_len, y_pool_p, n_prompt)

    q_lat, q_pe = _q_prep(cq, w_qn, w_qr, w_qr_sw, w_uk_t, cosf, sinf)
    o = _attn_prompt(q_lat, q_pe, ckv_b, kpe_b, w_uv_t, batch, seq, sm_scale)
    o_lat_s = _attn_sample(q_lat[:, n_prompt:].transpose(1, 0, 2), q_pe[:, n_prompt:].transpose(1, 0, 2),
                           ckv[n_prompt:, None, :], kpe[n_prompt:, None, :],
                           cache_ckv, cache_kpe_t, page_table, sm_scale)
    o = _o_proj(o_lat_s.transpose(1, 0, 2), w_uv_t, o, n_prompt)
    y = _mm2(y_pool, o, w_out0)

    x, h = _post(x, y, g[3], 1.0, g[4])
    g1 = norm_gains[1]
    x, h = _post(x, _ffn(h, wg, wu, wd, 0, 1), g[5], 0.5, g1[0])

    x, h = _post(x, _ffn(h, wg, wu, wd, 1, 0), g1[1], 0.5, g1[2])

    u = _mm(h, w_in1)
    gpc = _tile(groups, (S5_CHUNK_GROUPS, 8, 4, 2, 1))
    n_chunks = groups // gpc
    cw_st = gpc * n_state
    bbt_re, bbt_im, pw_re, pw_im = _s5_disc(lam_re, lam_im, log_dt, b_re.transpose(2, 0, 1), b_im.transpose(2, 0, 1))
    bd_re = _lane_repeated(bbt_re.transpose(1, 0, 2))
    bd_im = _lane_repeated(bbt_im.transpose(1, 0, 2))
    cd_re = _lane_repeated(c_re)
    cd_im = _lane_repeated(c_im)
    pw_re = pw_re.reshape(n_chunks, 1, cw_st)
    pw_im = pw_im.reshape(n_chunks, 1, cw_st)
    zero_h = jnp.zeros((batch, n_chunks, 1, cw_st), F32)
    z, zb, hr_p, hi_p = _s5_scan(u, 0, batch, seq, grp, n_state, bd_re, bd_im, pw_re, pw_im, cd_re, cd_im,
                                 d_skip, zero_h, zero_h)
    z, zb, hr_s, hi_s = _s5_step(u, n_prompt, bd, grp, n_state, bd_re, bd_im, pw_re, pw_im, cd_re, cd_im,
                                 d_skip, state_ssm_re.reshape(bd, groups * n_state),
                                 state_ssm_im.reshape(bd, groups * n_state), z, zb)
    y = _mm(_mm_glu(zb, w_glu, z), w_out1)

    x, h = _post(x, y, g1[3], 1.0, g1[4])
    xp, xs = _post_last(x, _ffn(h, wg, wu, wd, 1, 1), g1[5], 0.5, n_prompt)

    y_prompt = xp.reshape(batch, seq, d_model)
    y_sample = xs.reshape(bd, 1, d_model)
    pool_p = u_pool[:n_prompt].reshape(batch, seq, pool_width)[:, seq - pool_buf:]
    pool_s = jnp.concatenate([state_pool[:, 1:], u_pool[n_prompt:, None, :]], axis=1)
    ckv_p = ckv[:n_prompt].reshape(batch, seq, kv_lora)
    ckv_s = ckv[n_prompt:].reshape(bd, 1, kv_lora)
    kpe_p = kpe[:n_prompt].reshape(batch, seq, qk_rope)
    kpe_s = kpe[n_prompt:].reshape(bd, 1, qk_rope)
    return (y_prompt, y_sample, pool_p, pool_s, ckv_p, ckv_s, kpe_p, kpe_s,
            hr_p.reshape(batch, groups, n_state), hi_p.reshape(batch, groups, n_state),
            hr_s.reshape(bd, groups, n_state), hi_s.reshape(bd, groups, n_state))
```
